```python
import math
import jax, jax.numpy as jnp
from jax import lax
import numpy as np

D_MODEL = 2048
BATCH = 1
SEQ = 16384
DEPTH = 1

HEAD_DIM = 128
N_HEADS_DSA = 8
N_HEADS_FOX = 8
DSA_WIDTH = N_HEADS_DSA * HEAD_DIM
FOX_WIDTH = N_HEADS_FOX * HEAD_DIM
MIX_WIDTH = DSA_WIDTH + FOX_WIDTH
IDX_HEADS = 8
IDX_DIM = 64
TOPK_MAX = 256
N_BUCKETS = 32
MAX_DISTANCE = 128
Q_BLOCK = 128
D_FF = 5504
CONV_WIDTH = 3
EPS = 1e-6
NEG = -1e30

IN_SPLITS = (
    DSA_WIDTH, DSA_WIDTH, DSA_WIDTH,
    IDX_HEADS * IDX_DIM, IDX_DIM, IDX_HEADS,
    FOX_WIDTH, FOX_WIDTH, FOX_WIDTH,
    FOX_WIDTH, N_HEADS_FOX,
)
IN_WIDTH = sum(IN_SPLITS)

kernel_name = "hybrid_dsa_fox_convffn_adaln"


def rmsnorm(x, g):
    xf = x.astype(jnp.float32)
    y = xf * lax.rsqrt(jnp.mean(xf * xf, axis=-1, keepdims=True) + EPS)
    return (y * g.astype(jnp.float32)).astype(x.dtype)


def modulate(h, shift, scale):
    return h * (1 + scale[:, None, :]) + shift[:, None, :]


def t5_bucket(dist):
    max_exact = N_BUCKETS // 2
    d = jnp.maximum(dist, 0)
    df = jnp.maximum(d, 1).astype(jnp.float32)
    large = max_exact + (jnp.log(df / max_exact) / math.log(MAX_DISTANCE / max_exact)
                         * (N_BUCKETS - max_exact)).astype(jnp.int32)
    large = jnp.minimum(large, N_BUCKETS - 1)
    return jnp.where(d < max_exact, d, large)


def dsa_attention(q, k, v, q_idx, k_idx, w_idx, rel_bias):
    B, S, H, Dh = q.shape
    n_sel = min(TOPK_MAX, S // 4)
    key_pos = jnp.arange(S, dtype=jnp.int32)
    k_idx_f = k_idx.astype(jnp.float32)

    def block(i):
        start = i * Q_BLOCK
        qb = lax.dynamic_slice_in_dim(q, start, Q_BLOCK, axis=1)
        qib = lax.dynamic_slice_in_dim(q_idx, start, Q_BLOCK, axis=1)
        wb = lax.dynamic_slice_in_dim(w_idx, start, Q_BLOCK, axis=1)
        q_pos = start + jnp.arange(Q_BLOCK, dtype=jnp.int32)
        causal = key_pos[None, :] <= q_pos[:, None]
        idx_logits = jnp.einsum('bqhd,bsd->bqhs', qib.astype(jnp.float32), k_idx_f) * (IDX_DIM ** -0.5)
        score = jnp.einsum('bqhs,bqh->bqs', jax.nn.relu(idx_logits),
                           wb.astype(jnp.float32)) * (IDX_HEADS ** -0.5)
        score = jnp.where(causal[None], score, NEG)
        _, sel = lax.top_k(score, n_sel)
        valid = sel <= q_pos[None, :, None]
        k_sel = jax.vmap(lambda kk, ii: kk[ii])(k, sel)
        v_sel = jax.vmap(lambda vv, ii: vv[ii])(v, sel)
        logits = jnp.einsum('bqhd,bqkhd->bhqk', qb.astype(jnp.float32),
                            k_sel.astype(jnp.float32)) * (Dh ** -0.5)
        bucket = t5_bucket(q_pos[None, :, None] - sel)
        bias = rel_bias.astype(jnp.float32)[bucket]
        logits = logits + jnp.transpose(bias, (0, 3, 1, 2))
        logits = jnp.where(valid[:, None], logits, NEG)
        p = jax.nn.softmax(logits, axis=-1)
        out = jnp.einsum('bhqk,bqkhd->bqhd', p, v_sel.astype(jnp.float32))
        return out.astype(v.dtype)

    outs = lax.map(block, jnp.arange(S // Q_BLOCK))
    return jnp.transpose(outs, (1, 0, 2, 3, 4)).reshape(B, S, H, Dh)


def fox_attention(q, k, v, log_f):
    B, S, H, Dh = q.shape
    F = jnp.transpose(jnp.cumsum(log_f, axis=1), (0, 2, 1))
    key_pos = jnp.arange(S, dtype=jnp.int32)
    k_f = k.astype(jnp.float32)
    v_f = v.astype(jnp.float32)

    def block(i):
        start = i * Q_BLOCK
        qb = lax.dynamic_slice_in_dim(q, start, Q_BLOCK, axis=1)
        Fq = lax.dynamic_slice_in_dim(F, start, Q_BLOCK, axis=2)
        q_pos = start + jnp.arange(Q_BLOCK, dtype=jnp.int32)
        causal = key_pos[None, :] <= q_pos[:, None]
        logits = jnp.einsum('bqhd,bshd->bhqs', qb.astype(jnp.float32), k_f) * (Dh ** -0.5)
        logits = logits + (Fq[..., None] - F[:, :, None, :])
        logits = jnp.where(causal[None, None], logits, NEG)
        p = jax.nn.softmax(logits, axis=-1)
        out = jnp.einsum('bhqs,bshd->bqhd', p, v_f)
        return out.astype(v.dtype)

    outs = lax.map(block, jnp.arange(S // Q_BLOCK))
    return jnp.transpose(outs, (1, 0, 2, 3, 4)).reshape(B, S, H, Dh)


def causal_dwconv(u, w, b):
    C = u.shape[-1]
    y = lax.conv_general_dilated(
        u, w[:, None, :].astype(u.dtype), window_strides=(1,),
        padding=[(CONV_WIDTH - 1, 0)], dimension_numbers=('NWC', 'WIO', 'NWC'),
        feature_group_count=C)
    return y + b


def setup_inputs(seed: int = 0) -> dict:
    key = jax.random.key(seed)
    ks = jax.random.split(key, 16)
    D = D_MODEL
    nrm = jax.random.normal
    x = nrm(ks[0], (BATCH, SEQ, D), jnp.float32)
    c = nrm(ks[1], (BATCH, D), jnp.float32)
    rel_bias = 0.5 * nrm(ks[2], (N_BUCKETS, N_HEADS_DSA), jnp.float32)
    w_ada = nrm(ks[3], (DEPTH, D, 6 * D), jnp.float32) * (0.5 * D ** -0.5)
    b_ada = 0.01 * nrm(ks[4], (DEPTH, 6 * D), jnp.float32)
    g_attn = 1.0 + 0.02 * nrm(ks[5], (DEPTH, D), jnp.float32)
    w_in = nrm(ks[6], (DEPTH, D, IN_WIDTH), jnp.float32) * D ** -0.5
    b_forget = jax.random.uniform(ks[7], (DEPTH, N_HEADS_FOX), jnp.float32, minval=1.0, maxval=4.0)
    w_out = nrm(ks[8], (DEPTH, MIX_WIDTH, D), jnp.float32) * MIX_WIDTH ** -0.5
    g_mlp = 1.0 + 0.02 * nrm(ks[9], (DEPTH, D), jnp.float32)
    w_up = nrm(ks[10], (DEPTH, D, 2 * D_FF), jnp.float32) * D ** -0.5
    conv_w = nrm(ks[11], (DEPTH, CONV_WIDTH, 2 * D_FF), jnp.float32) * CONV_WIDTH ** -0.5
    conv_b = 0.01 * nrm(ks[12], (DEPTH, 2 * D_FF), jnp.float32)
    w_down = nrm(ks[13], (DEPTH, D_FF, D), jnp.float32) * D_FF ** -0.5
    g_final = 1.0 + 0.02 * nrm(ks[14], (D,), jnp.float32)
    return {"x": x, "c": c, "rel_bias": rel_bias, "w_ada": w_ada, "b_ada": b_ada,
            "g_attn": g_attn, "w_in": w_in, "b_forget": b_forget, "w_out": w_out,
            "g_mlp": g_mlp, "w_up": w_up, "conv_w": conv_w, "conv_b": conv_b,
            "w_down": w_down, "g_final": g_final}


def reference(x, c, rel_bias, w_ada, b_ada, g_attn, w_in, b_forget, w_out,
              g_mlp, w_up, conv_w, conv_b, w_down, g_final):
    B, S, D = x.shape
    split_at = np.cumsum(IN_SPLITS)[:-1].tolist()
    c_act = jax.nn.silu(c)
    for l in range(DEPTH):
        mod = c_act @ w_ada[l] + b_ada[l]
        shift_a, scale_a, gate_a, shift_m, scale_m, gate_m = jnp.split(mod, 6, axis=-1)

        h = modulate(rmsnorm(x, g_attn[l]), shift_a, scale_a)
        proj = h @ w_in[l]
        (qa, ka, va, qi, ki, wi, qb, kb, vb, gb, fb) = jnp.split(proj, split_at, axis=-1)
        qa = qa.reshape(B, S, N_HEADS_DSA, HEAD_DIM)
        ka = ka.reshape(B, S, N_HEADS_DSA, HEAD_DIM)
        va = va.reshape(B, S, N_HEADS_DSA, HEAD_DIM)
        qi = qi.reshape(B, S, IDX_HEADS, IDX_DIM)
        o_a = dsa_attention(qa, ka, va, qi, ki, wi, rel_bias)

        qb = qb.reshape(B, S, N_HEADS_FOX, HEAD_DIM)
        kb = kb.reshape(B, S, N_HEADS_FOX, HEAD_DIM)
        vb = vb.reshape(B, S, N_HEADS_FOX, HEAD_DIM)
        log_f = jax.nn.log_sigmoid((fb + b_forget[l]).astype(jnp.float32))
        o_b = fox_attention(qb, kb, vb, log_f)
        o_b = o_b.reshape(B, S, FOX_WIDTH) * jax.nn.sigmoid(gb)

        mixed = jnp.concatenate([o_a.reshape(B, S, DSA_WIDTH), o_b], axis=-1)
        x = x + gate_a[:, None, :] * (mixed @ w_out[l])

        h = modulate(rmsnorm(x, g_mlp[l]), shift_m, scale_m)
        u = causal_dwconv(h @ w_up[l], conv_w[l], conv_b[l])
        u_gate, u_val = jnp.split(u, 2, axis=-1)
        x = x + gate_m[:, None, :] * ((jax.nn.silu(u_gate) * u_val) @ w_down[l])

    return rmsnorm(x, g_final)
```

```python
import functools
import math

import jax
import jax.numpy as jnp
import numpy as np
from jax import lax
from jax.experimental import pallas as pl
from jax.experimental.pallas import tpu as pltpu

HEAD_DIM = 128
N_HEADS = 8
GROUP_WIDTH = N_HEADS * HEAD_DIM
IDX_HEADS = 8
IDX_DIM = 64
TOPK_MAX = 256
N_BUCKETS = 32
MAX_DISTANCE = 128
CONV_WIDTH = 3
EPS = 1e-6
NEG = -1e30
NEG_MASK = -(2.0 ** 100)

LANES = 128
SUBLANES = 8
VMEM_LIMIT = 56 * 1024 * 1024

ROW_TILE = 512
ATT_TILE = 256
SEL_ROWS = 128
FF_TILE = 512
CONV_HALO = 16
SMALL_WIDTH = 640


def _cparams(sem):
    return pltpu.CompilerParams(dimension_semantics=sem, vmem_limit_bytes=VMEM_LIMIT)


def _ada_kernel(c_ref, w_ref, b_ref, o_ref):
    d = w_ref.shape[0]
    rows = 256

    def body(r, acc):
        sl = pl.ds(pl.multiple_of(r * rows, rows), rows)
        cc = c_ref[sl, :]
        ca = cc / (1.0 + jnp.exp(-cc))
        prod = w_ref[sl, :] * ca
        return acc + prod.reshape(rows // SUBLANES, SUBLANES, -1).sum(axis=0)

    acc = lax.fori_loop(0, d // rows, body, jnp.zeros((SUBLANES, w_ref.shape[1]), jnp.float32))
    o_ref[...] = jnp.sum(acc, axis=0, keepdims=True) + b_ref[...]


def _ada(c_col, w, b_row):
    d, n = w.shape
    tn = 1024
    return pl.pallas_call(
        _ada_kernel,
        grid=(n // tn,),
        in_specs=[pl.BlockSpec((d, 1), lambda j: (0, 0)),
                  pl.BlockSpec((d, tn), lambda j: (0, j)),
                  pl.BlockSpec((1, tn), lambda j: (0, j))],
        out_specs=pl.BlockSpec((1, tn), lambda j: (0, j)),
        out_shape=jax.ShapeDtypeStruct((1, n), jnp.float32),
        compiler_params=_cparams(("arbitrary",)),
        name="adaln",
    )(c_col, w, b_row)


def _norm_mod(x, g, shift, scale):
    y = x * lax.rsqrt(jnp.mean(x * x, axis=-1, keepdims=True) + EPS)
    return (y * g) * (1.0 + scale) + shift


def _inproj_kernel(x_ref, g_ref, sh_ref, sc_ref, wm_ref, ws_ref, om_ref, oq_ref, os_ref, h_ref):
    j = pl.program_id(1)

    @pl.when(j == 0)
    def _():
        h = _norm_mod(x_ref[...], g_ref[...], sh_ref[...], sc_ref[...]).astype(jnp.bfloat16)
        h_ref[...] = h
        small = jnp.dot(h, ws_ref[...], preferred_element_type=jnp.float32)
        nq = IDX_HEADS * IDX_DIM
        oq_ref[...] = small[:, :nq].astype(jnp.bfloat16)
        os_ref[...] = small[:, nq:]

    om_ref[...] = jnp.dot(h_ref[...], wm_ref[...],
                          preferred_element_type=jnp.float32).astype(jnp.bfloat16)


def _inproj(x, g, shift, scale, w_main, w_small):
    s, d = x.shape
    n_main = w_main.shape[1]
    tm, tn = ROW_TILE, GROUP_WIDTH
    nq = IDX_HEADS * IDX_DIM
    row = lambda i, j: (0, 0)
    return pl.pallas_call(
        _inproj_kernel,
        grid=(s // tm, n_main // tn),
        in_specs=[pl.BlockSpec((tm, d), lambda i, j: (i, 0)),
                  pl.BlockSpec((1, d), row), pl.BlockSpec((1, d), row), pl.BlockSpec((1, d), row),
                  pl.BlockSpec((d, tn), lambda i, j: (0, j)),
                  pl.BlockSpec((d, SMALL_WIDTH), row)],
        out_specs=[pl.BlockSpec((tm, tn), lambda i, j: (i, j)),
                   pl.BlockSpec((tm, nq), lambda i, j: (i, 0)),
                   pl.BlockSpec((tm, SMALL_WIDTH - nq), lambda i, j: (i, 0))],
        out_shape=[jax.ShapeDtypeStruct((s, n_main), jnp.bfloat16),
                   jax.ShapeDtypeStruct((s, nq), jnp.bfloat16),
                   jax.ShapeDtypeStruct((s, SMALL_WIDTH - nq), jnp.float32)],
        scratch_shapes=[pltpu.VMEM((tm, d), jnp.bfloat16)],
        compiler_params=_cparams(("parallel", "arbitrary")),
        name="inproj",
    )(x, g, shift, scale, w_main, w_small)


def _fcum_kernel(fb_ref, b_ref, o_ref):
    r = fb_ref.shape[1]
    hi = lax.Precision.HIGHEST
    ci = lax.broadcasted_iota(jnp.int32, (LANES, LANES), 0)
    cj = lax.broadcasted_iota(jnp.int32, (LANES, LANES), 1)
    upper = (ci <= cj).astype(jnp.float32)
    ri = lax.broadcasted_iota(jnp.int32, (r, r), 0)
    rj = lax.broadcasted_iota(jnp.int32, (r, r), 1)
    strict_lower = (rj < ri).astype(jnp.float32)
    for h in range(N_HEADS):
        z = fb_ref[h] + b_ref[h]
        lf = jnp.minimum(z, 0.0) - jnp.log(1.0 + jnp.exp(-jnp.abs(z)))
        within = jnp.dot(lf, upper, precision=hi, preferred_element_type=jnp.float32)
        before = jnp.dot(strict_lower, within, precision=hi, preferred_element_type=jnp.float32)
        o_ref[h] = within + before[:, LANES - 1:LANES]


def _fcum(fb_t, b_forget):
    h, r, _ = fb_t.shape
    return pl.pallas_call(
        _fcum_kernel,
        in_specs=[pl.BlockSpec(memory_space=pltpu.VMEM),
                  pl.BlockSpec(memory_space=pltpu.SMEM)],
        out_specs=pl.BlockSpec(memory_space=pltpu.VMEM),
        out_shape=jax.ShapeDtypeStruct((h, r, LANES), jnp.float32),
        name="forget_cumsum",
    )(fb_t, b_forget)


def _key_to_float(key):
    bits = key ^ ((key >> 31) & jnp.int32(0x7FFFFFFF))
    return pltpu.bitcast(bits, jnp.float32)


def _select_kernel(qi_ref, kt_ref, sm_ref, mask_ref, s_ref, *, seq, n_sel):
    tq = qi_ref.shape[0]
    i = pl.program_id(0)
    n_chunks = i + 1
    n_rest = (seq - n_chunks * tq).astype(jnp.float32)
    row0 = i * tq
    wts = sm_ref[:, IDX_DIM:IDX_DIM + IDX_HEADS] * (IDX_HEADS ** -0.5)

    def score_chunk(c, _):
        col0 = pl.multiple_of(c * tq, tq)
        kc = kt_ref[:, pl.ds(col0, tq)]
        for r0 in range(0, tq, SEL_ROWS):
            acc = jnp.zeros((SEL_ROWS, tq), jnp.float32)
            for h in range(IDX_HEADS):
                lg = jnp.dot(qi_ref[r0:r0 + SEL_ROWS, h * IDX_DIM:(h + 1) * IDX_DIM], kc,
                             preferred_element_type=jnp.float32)
                acc = acc + jnp.maximum(lg, 0.0) * wts[r0:r0 + SEL_ROWS, h:h + 1]
            rows = row0 + r0 + lax.broadcasted_iota(jnp.int32, (SEL_ROWS, tq), 0)
            cols = col0 + lax.broadcasted_iota(jnp.int32, (SEL_ROWS, tq), 1)
            s_ref[r0:r0 + SEL_ROWS, pl.ds(col0, tq)] = jnp.where(cols <= rows, acc, NEG)
        return 0

    lax.fori_loop(0, n_chunks, score_chunk, 0)

    def sweep(r0, pred):
        def body(c, acc):
            col0 = pl.multiple_of(c * tq, tq)
            for k in range(tq // LANES):
                slab = s_ref[r0:r0 + SEL_ROWS, pl.ds(col0 + k * LANES, LANES)]
                acc = acc + jnp.where(pred(slab, col0 + k * LANES), 1.0, 0.0)
            return acc
        acc = lax.fori_loop(0, n_chunks, body, jnp.zeros((SEL_ROWS, LANES), jnp.float32))
        return jnp.sum(acc, axis=-1, keepdims=True)

    def lanes(col):
        return jnp.broadcast_to(col, (SEL_ROWS, LANES))

    lane_iota = lax.broadcasted_iota(jnp.int32, (SEL_ROWS, LANES), 1)
    n_idx_bits = max(1, (seq - 1).bit_length())
    for r0 in range(0, tq, SEL_ROWS):
        def bit_step(p, lo):
            cand = lo + lax.shift_left(jnp.int32(1), 31 - p)
            cf = _key_to_float(cand)
            cfb = lanes(cf)
            cnt = sweep(r0, lambda x, col0: x >= cfb) + jnp.where(NEG >= cf, n_rest, 0.0)
            return jnp.where(cnt >= n_sel, cand, lo)

        lo = lax.fori_loop(0, 32, bit_step, jnp.full((SEL_ROWS, 1), -2 ** 31, jnp.int32))
        thr = _key_to_float(lo)
        thrb = lanes(thr)

        cnt_gt = sweep(r0, lambda x, col0: x > thrb) + jnp.where(NEG > thr, n_rest, 0.0)
        cnt_eq = sweep(r0, lambda x, col0: x == thrb)
        need = n_sel - cnt_gt
        all_ties = jnp.full((SEL_ROWS, 1), seq - 1, jnp.int32)

        def tie_search(_):
            def idx_step(p, jlo):
                cand = jlo + lax.shift_left(jnp.int32(1), n_idx_bits - 1 - p)
                candb = lanes(cand)
                cnt = sweep(r0, lambda x, col0: (x == thrb) & (lane_iota + col0 < candb))
                return jnp.where(cnt < need, cand, jlo)
            return lax.fori_loop(0, n_idx_bits, idx_step, jnp.zeros((SEL_ROWS, 1), jnp.int32))

        excess = jnp.max(jnp.where(cnt_eq > need, 1.0, 0.0)) > 0.0
        jmax = lax.cond(excess, tie_search, lambda _: all_ties, 0)

        def mask_chunk(c, _):
            col0 = pl.multiple_of(c * tq, tq)
            x = s_ref[r0:r0 + SEL_ROWS, pl.ds(col0, tq)]
            rows = row0 + r0 + lax.broadcasted_iota(jnp.int32, x.shape, 0)
            cols = col0 + lax.broadcasted_iota(jnp.int32, x.shape, 1)
            keep = ((x > thr) | ((x == thr) & (cols <= jmax))) & (cols <= rows)
            mask_ref[r0:r0 + SEL_ROWS, pl.ds(col0, tq)] = jnp.where(keep, 0.0, NEG_MASK).astype(jnp.bfloat16)
            return 0

        lax.fori_loop(0, n_chunks, mask_chunk, 0)

    def fill_chunk(c, _):
        col0 = pl.multiple_of(c * tq, tq)
        mask_ref[:, pl.ds(col0, tq)] = jnp.full((tq, tq), NEG_MASK, jnp.bfloat16)
        return 0

    lax.fori_loop(n_chunks, seq // tq, fill_chunk, 0)


def _select(qi, k_t, small, n_sel):
    s = qi.shape[0]
    tq = ATT_TILE
    return pl.pallas_call(
        functools.partial(_select_kernel, seq=s, n_sel=n_sel),
        grid=(s // tq,),
        in_specs=[pl.BlockSpec((tq, qi.shape[1]), lambda i: (i, 0)),
                  pl.BlockSpec(k_t.shape, lambda i: (0, 0)),
                  pl.BlockSpec((tq, small.shape[1]), lambda i: (i, 0))],
        out_specs=pl.BlockSpec((tq, s), lambda i: (i, 0)),
        out_shape=jax.ShapeDtypeStruct((s, s), jnp.bfloat16),
        scratch_shapes=[pltpu.VMEM((tq, s), jnp.float32)],
        compiler_params=_cparams(("parallel",)),
        name="dsa_select",
    )(qi, k_t, small)


def _softmax_step(h, s, v, m_ref, l_ref, acc_ref):
    m_prev = m_ref[h]
    m_new = jnp.maximum(m_prev, jnp.max(s, axis=-1, keepdims=True))
    alpha = jnp.exp(m_prev - m_new)
    p = jnp.exp(s - m_new)
    l_ref[h] = alpha * l_ref[h] + jnp.sum(p, axis=-1, keepdims=True)
    acc_ref[h] = alpha * acc_ref[h] + jnp.dot(p.astype(jnp.bfloat16), v,
                                              preferred_element_type=jnp.float32)
    m_ref[h] = m_new


def _init_softmax(m_ref, l_ref, acc_ref):
    m_ref[...] = jnp.full(m_ref.shape, NEG_MASK, jnp.float32)
    l_ref[...] = jnp.zeros(l_ref.shape, jnp.float32)
    acc_ref[...] = jnp.zeros(acc_ref.shape, jnp.float32)


def _qk(q, k):
    return lax.dot_general(q, k, (((1,), (1,)), ((), ())), preferred_element_type=jnp.float32)


def _dsa_kernel(qt_ref, st_ref, q_ref, k_ref, v_ref, mask_ref, bias_ref, o_ref, m_ref, l_ref, acc_ref):
    p = pl.program_id(0)
    qi, si = qt_ref[p], st_ref[p]

    @pl.when(si == 0)
    def _():
        _init_softmax(m_ref, l_ref, acc_ref)

    maskf = mask_ref[...].astype(jnp.float32)
    for h in range(N_HEADS):
        hs = slice(h * HEAD_DIM, (h + 1) * HEAD_DIM)
        s = _qk(q_ref[:, hs], k_ref[:, hs]) + bias_ref[h, 0] + maskf
        _softmax_step(h, s, v_ref[:, hs], m_ref, l_ref, acc_ref)

    @pl.when(si == qi)
    def _():
        for h in range(N_HEADS):
            hs = slice(h * HEAD_DIM, (h + 1) * HEAD_DIM)
            o_ref[:, hs] = (acc_ref[h] / l_ref[h]).astype(o_ref.dtype)


def _fox_kernel(qt_ref, st_ref, q_ref, k_ref, v_ref, g_ref, fq_ref, fs_ref, o_ref, m_ref, l_ref, acc_ref):
    p = pl.program_id(0)
    qi, si = qt_ref[p], st_ref[p]
    tq, ts = q_ref.shape[0], k_ref.shape[0]

    @pl.when(si == 0)
    def _():
        _init_softmax(m_ref, l_ref, acc_ref)

    rows = qi * tq + lax.broadcasted_iota(jnp.int32, (tq, ts), 0)
    cols = si * ts + lax.broadcasted_iota(jnp.int32, (tq, ts), 1)
    causal = cols <= rows
    for h in range(N_HEADS):
        hs = slice(h * HEAD_DIM, (h + 1) * HEAD_DIM)
        decay = fq_ref[h:h + 1, 0:1] - fs_ref[h:h + 1, :]
        s = jnp.where(causal, _qk(q_ref[:, hs], k_ref[:, hs]) + decay, NEG_MASK)
        _softmax_step(h, s, v_ref[:, hs], m_ref, l_ref, acc_ref)

    @pl.when(si == qi)
    def _():
        for h in range(N_HEADS):
            hs = slice(h * HEAD_DIM, (h + 1) * HEAD_DIM)
            g = g_ref[:, hs].astype(jnp.float32)
            o_ref[:, hs] = (acc_ref[h] / l_ref[h] / (1.0 + jnp.exp(-g))).astype(o_ref.dtype)


def _pair_tables(nq):
    qt = np.concatenate([np.full(q + 1, q, np.int32) for q in range(nq)])
    st = np.concatenate([np.arange(q + 1, dtype=np.int32) for q in range(nq)])
    return jnp.asarray(qt), jnp.asarray(st)


def _att_scratch(tq):
    return [pltpu.VMEM((N_HEADS, tq, 1), jnp.float32),
            pltpu.VMEM((N_HEADS, tq, 1), jnp.float32),
            pltpu.VMEM((N_HEADS, tq, HEAD_DIM), jnp.float32)]


def _dsa(proj, mask, bias_tiles):
    s = proj.shape[0]
    t = ATT_TILE
    qt, st = _pair_tables(s // t)
    w = GROUP_WIDTH
    grid_spec = pltpu.PrefetchScalarGridSpec(
        num_scalar_prefetch=2,
        grid=(qt.shape[0],),
        in_specs=[pl.BlockSpec((t, w), lambda p, qt, st: (qt[p], 0)),
                  pl.BlockSpec((t, w), lambda p, qt, st: (st[p], 1)),
                  pl.BlockSpec((t, w), lambda p, qt, st: (st[p], 2)),
                  pl.BlockSpec((t, t), lambda p, qt, st: (qt[p], st[p])),
                  pl.BlockSpec((N_HEADS, 1, t, t),
                               lambda p, qt, st: (0, jnp.minimum(qt[p] - st[p], 2), 0, 0))],
        out_specs=pl.BlockSpec((t, w), lambda p, qt, st: (qt[p], 0)),
        scratch_shapes=_att_scratch(t))
    return pl.pallas_call(
        _dsa_kernel, grid_spec=grid_spec,
        out_shape=jax.ShapeDtypeStruct((s, w), jnp.bfloat16),
        compiler_params=_cparams(("arbitrary",)),
        name="dsa_attention",
    )(qt, st, proj, proj, proj, mask, bias_tiles)


def _fox(proj, f_rows):
    s = proj.shape[0]
    t = ATT_TILE
    qt, st = _pair_tables(s // t)
    w = GROUP_WIDTH
    grid_spec = pltpu.PrefetchScalarGridSpec(
        num_scalar_prefetch=2,
        grid=(qt.shape[0],),
        in_specs=[pl.BlockSpec((t, w), lambda p, qt, st: (qt[p], 3)),
                  pl.BlockSpec((t, w), lambda p, qt, st: (st[p], 4)),
                  pl.BlockSpec((t, w), lambda p, qt, st: (st[p], 5)),
                  pl.BlockSpec((t, w), lambda p, qt, st: (qt[p], 6)),
                  pl.BlockSpec((N_HEADS, t), lambda p, qt, st: (0, qt[p])),
                  pl.BlockSpec((N_HEADS, t), lambda p, qt, st: (0, st[p]))],
        out_specs=pl.BlockSpec((t, w), lambda p, qt, st: (qt[p], 0)),
        scratch_shapes=_att_scratch(t))
    return pl.pallas_call(
        _fox_kernel, grid_spec=grid_spec,
        out_shape=jax.ShapeDtypeStruct((s, w), jnp.bfloat16),
        compiler_params=_cparams(("arbitrary",)),
        name="fox_attention",
    )(qt, st, proj, proj, proj, proj, f_rows, f_rows)


def _outproj_kernel(x_ref, oa_ref, ob_ref, wa_ref, wb_ref, gate_ref, o_ref):
    y = jnp.dot(oa_ref[...], wa_ref[...], preferred_element_type=jnp.float32)
    y = y + jnp.dot(ob_ref[...], wb_ref[...], preferred_element_type=jnp.float32)
    o_ref[...] = x_ref[...] + gate_ref[...] * y


def _outproj(x, o_a, o_b, w_a, w_b, gate):
    s, d = x.shape
    tm, tn = ROW_TILE, 1024
    w = GROUP_WIDTH
    return pl.pallas_call(
        _outproj_kernel,
        grid=(s // tm, d // tn),
        in_specs=[pl.BlockSpec((tm, tn), lambda i, j: (i, j)),
                  pl.BlockSpec((tm, w), lambda i, j: (i, 0)),
                  pl.BlockSpec((tm, w), lambda i, j: (i, 0)),
                  pl.BlockSpec((w, tn), lambda i, j: (0, j)),
                  pl.BlockSpec((w, tn), lambda i, j: (0, j)),
                  pl.BlockSpec((1, tn), lambda i, j: (0, j))],
        out_specs=pl.BlockSpec((tm, tn), lambda i, j: (i, j)),
        out_shape=jax.ShapeDtypeStruct((s, d), jnp.float32),
        compiler_params=_cparams(("parallel", "arbitrary")),
        name="outproj",
    )(x, o_a, o_b, w_a, w_b, gate)


def _conv(y_ref, cw_ref, cb_ref, tm):
    h = CONV_HALO
    return (cw_ref[2:3, :] * y_ref[h:h + tm, :] + cw_ref[1:2, :] * y_ref[h - 1:h - 1 + tm, :]
            + cw_ref[0:1, :] * y_ref[h - 2:h - 2 + tm, :] + cb_ref[...])


def _ffn_kernel(x_ref, xh_ref, g_ref, sh_ref, sc_ref, wg_ref, wv_ref, cwg_ref, cwv_ref, cbg_ref, cbv_ref,
                wd_ref, gate_ref, gf_ref, o_ref, h_ref, yg_ref, yv_ref, acc_ref):
    i, f = pl.program_id(0), pl.program_id(1)
    tm = x_ref.shape[0]

    @pl.when(f == 0)
    def _():
        halo = _norm_mod(xh_ref[...], g_ref[...], sh_ref[...], sc_ref[...])
        h_ref[0:CONV_HALO, :] = jnp.where(i > 0, halo, 0.0).astype(jnp.bfloat16)
        h_ref[CONV_HALO:, :] = _norm_mod(x_ref[...], g_ref[...], sh_ref[...], sc_ref[...]).astype(jnp.bfloat16)
        acc_ref[...] = jnp.zeros(acc_ref.shape, jnp.float32)

    hb = h_ref[...]
    yg_ref[...] = jnp.dot(hb, wg_ref[...], preferred_element_type=jnp.float32)
    yv_ref[...] = jnp.dot(hb, wv_ref[...], preferred_element_type=jnp.float32)
    ug = _conv(yg_ref, cwg_ref, cbg_ref, tm)
    uv = _conv(yv_ref, cwv_ref, cbv_ref, tm)
    a = (ug / (1.0 + jnp.exp(-ug))) * uv
    acc_ref[...] += jnp.dot(a.astype(jnp.bfloat16), wd_ref[...], preferred_element_type=jnp.float32)

    @pl.when(f == pl.num_programs(1) - 1)
    def _():
        x2 = x_ref[...] + gate_ref[...] * acc_ref[...]
        y = x2 * lax.rsqrt(jnp.mean(x2 * x2, axis=-1, keepdims=True) + EPS)
        o_ref[...] = y * gf_ref[...]


def _ffn(x, g, shift, scale, w_up, conv_w, conv_b, w_down, gate, g_final):
    s, d = x.shape
    fp = w_down.shape[0]
    tm, tf = ROW_TILE, FF_TILE
    nf = fp // tf
    hb = tm // CONV_HALO
    row = lambda i, f: (0, 0)
    return pl.pallas_call(
        _ffn_kernel,
        grid=(s // tm, nf),
        in_specs=[pl.BlockSpec((tm, d), lambda i, f: (i, 0)),
                  pl.BlockSpec((CONV_HALO, d), lambda i, f: (jnp.maximum(i * hb - 1, 0), 0)),
                  pl.BlockSpec((1, d), row), pl.BlockSpec((1, d), row), pl.BlockSpec((1, d), row),
                  pl.BlockSpec((d, tf), lambda i, f: (0, f)),
                  pl.BlockSpec((d, tf), lambda i, f: (0, f + nf)),
                  pl.BlockSpec((CONV_WIDTH, tf), lambda i, f: (0, f)),
                  pl.BlockSpec((CONV_WIDTH, tf), lambda i, f: (0, f + nf)),
                  pl.BlockSpec((1, tf), lambda i, f: (0, f)),
                  pl.BlockSpec((1, tf), lambda i, f: (0, f + nf)),
                  pl.BlockSpec((tf, d), lambda i, f: (f, 0)),
                  pl.BlockSpec((1, d), row), pl.BlockSpec((1, d), row)],
        out_specs=pl.BlockSpec((tm, d), lambda i, f: (i, 0)),
        out_shape=jax.ShapeDtypeStruct((s, d), jnp.float32),
        scratch_shapes=[pltpu.VMEM((tm + CONV_HALO, d), jnp.bfloat16),
                        pltpu.VMEM((tm + CONV_HALO, tf), jnp.float32),
                        pltpu.VMEM((tm + CONV_HALO, tf), jnp.float32),
                        pltpu.VMEM((tm, d), jnp.float32)],
        compiler_params=_cparams(("parallel", "arbitrary")),
        name="conv_ffn",
    )(x, x, g, shift, scale, w_up, w_up, conv_w, conv_w, conv_b, conv_b, w_down, gate, g_final)


def _t5_bias_tiles(rel_bias, t):
    max_exact = N_BUCKETS // 2
    d = jnp.arange(3 * t, dtype=jnp.int32)
    df = jnp.maximum(d, 1).astype(jnp.float32)
    large = max_exact + (jnp.log(df / max_exact) / math.log(MAX_DISTANCE / max_exact)
                         * (N_BUCKETS - max_exact)).astype(jnp.int32)
    bucket = jnp.where(d < max_exact, d, jnp.minimum(large, N_BUCKETS - 1))
    table = rel_bias.astype(jnp.float32)[bucket]
    i = jnp.arange(t)[:, None]
    j = jnp.arange(t)[None, :]
    dist = jnp.maximum(jnp.arange(3)[:, None, None] * t + i - j, 0)
    return jnp.transpose(table[dist], (3, 0, 1, 2))


def _pad_cols(a, n):
    return jnp.pad(a, ((0, 0), (0, n - a.shape[1])))


def kernel(x, c, rel_bias, w_ada, b_ada, g_attn, w_in, b_forget, w_out, g_mlp, w_up, conv_w, conv_b,
           w_down, g_final):
    b, s, d = x.shape
    assert b == 1 and s % ROW_TILE == 0 and s % ATT_TILE == 0 and ATT_TILE >= MAX_DISTANCE
    depth = w_ada.shape[0]
    assert depth == 1, "the fused MLP kernel applies the final norm, so it must be the last layer"
    d_ff = w_down.shape[1]
    ff_pad = -(-d_ff // FF_TILE) * FF_TILE
    n_sel = min(TOPK_MAX, s // 4)
    gw, nq = GROUP_WIDTH, IDX_HEADS * IDX_DIM
    bf = jnp.bfloat16
    xs = x[0]
    c_col = c.reshape(d, 1)
    bias_tiles = _t5_bias_tiles(rel_bias, ATT_TILE)

    for l in range(depth):
        mod = _ada(c_col, w_ada[l], b_ada[l][None, :])
        shift_a, scale_a, gate_a, shift_m, scale_m, gate_m = [mod[:, k * d:(k + 1) * d] for k in range(6)]

        wi = w_in[l]
        o = np.cumsum([0, gw, gw, gw, nq, IDX_DIM, IDX_HEADS, gw, gw, gw, gw, N_HEADS])
        seg = lambda k: wi[:, o[k]:o[k + 1]]
        w_main = jnp.concatenate([seg(0) * HEAD_DIM ** -0.5, seg(1), seg(2),
                                  seg(6) * HEAD_DIM ** -0.5, seg(7), seg(8), seg(9)], axis=1).astype(bf)
        w_small = _pad_cols(jnp.concatenate([seg(3) * IDX_DIM ** -0.5, seg(4), seg(5), seg(10)], axis=1),
                            SMALL_WIDTH).astype(bf)

        proj, qi, small = _inproj(xs, g_attn[l][None, :], shift_a, scale_a, w_main, w_small)

        k_t = small[:, :IDX_DIM].T.astype(bf)
        fb_t = small[:, IDX_DIM + IDX_HEADS:IDX_DIM + IDX_HEADS + N_HEADS].T
        f_rows = _fcum(fb_t.reshape(N_HEADS, s // LANES, LANES), b_forget[l]).reshape(N_HEADS, s)

        mask = _select(qi, k_t, small, n_sel)
        o_a = _dsa(proj, mask, bias_tiles)
        o_b = _fox(proj, f_rows)

        wo = w_out[l].astype(bf)
        x1 = _outproj(xs, o_a, o_b, wo[:gw], wo[gw:], gate_a)

        wu = w_up[l]
        w_up_p = jnp.concatenate([_pad_cols(wu[:, :d_ff], ff_pad), _pad_cols(wu[:, d_ff:], ff_pad)], axis=1).astype(bf)
        cw = conv_w[l]
        cw_p = jnp.concatenate([_pad_cols(cw[:, :d_ff], ff_pad), _pad_cols(cw[:, d_ff:], ff_pad)], axis=1)
        cb = conv_b[l][None, :]
        cb_p = jnp.concatenate([_pad_cols(cb[:, :d_ff], ff_pad), _pad_cols(cb[:, d_ff:], ff_pad)], axis=1)
        w_down_p = jnp.pad(w_down[l], ((0, ff_pad - d_ff), (0, 0))).astype(bf)

        xs = _ffn(x1, g_mlp[l][None, :], shift_m, scale_m, w_up_p, cw_p, cb_p, w_down_p, gate_m,
                  g_final[None, :])

    return xs[None]
```

```python
import functools
import math

import jax
import jax.numpy as jnp
import numpy as np
from jax import lax
from jax.experimental import pallas as pl
from jax.experimental.pallas import tpu as pltpu

HEAD_DIM = 128
N_HEADS = 8
GROUP_WIDTH = N_HEADS * HEAD_DIM
IDX_HEADS = 8
IDX_DIM = 64
TOPK_MAX = 256
N_BUCKETS = 32
MAX_DISTANCE = 128
CONV_WIDTH = 3
EPS = 1e-6
NEG = -1e30
NEG_MASK = -(2.0 ** 100)

LANES = 128
SUBLANES = 8
VMEM_LIMIT = 56 * 1024 * 1024

ROW_TILE = 512
ATT_TILE = 256
FOX_TILE = 512
LOG2E = math.log2(math.e)
SEL_ROWS = 128
FF_TILE = 512
CONV_HALO = 16
SMALL_WIDTH = 640


def _cparams(sem):
    return pltpu.CompilerParams(dimension_semantics=sem, vmem_limit_bytes=VMEM_LIMIT)


def _ada_kernel(c_ref, w_ref, b_ref, o_ref):
    d = w_ref.shape[0]
    rows = 256

    def body(r, acc):
        sl = pl.ds(pl.multiple_of(r * rows, rows), rows)
        cc = c_ref[sl, :]
        ca = cc / (1.0 + jnp.exp(-cc))
        prod = w_ref[sl, :] * ca
        return acc + prod.reshape(rows // SUBLANES, SUBLANES, -1).sum(axis=0)

    acc = lax.fori_loop(0, d // rows, body, jnp.zeros((SUBLANES, w_ref.shape[1]), jnp.float32))
    o_ref[...] = jnp.sum(acc, axis=0, keepdims=True) + b_ref[...]


def _ada(c_col, w, b_row):
    d, n = w.shape
    tn = 1024
    return pl.pallas_call(
        _ada_kernel,
        grid=(n // tn,),
        in_specs=[pl.BlockSpec((d, 1), lambda j: (0, 0)),
                  pl.BlockSpec((d, tn), lambda j: (0, j)),
                  pl.BlockSpec((1, tn), lambda j: (0, j))],
        out_specs=pl.BlockSpec((1, tn), lambda j: (0, j)),
        out_shape=jax.ShapeDtypeStruct((1, n), jnp.float32),
        compiler_params=_cparams(("arbitrary",)),
        name="adaln",
    )(c_col, w, b_row)


def _norm_mod(x, g, shift, scale):
    y = x * lax.rsqrt(jnp.mean(x * x, axis=-1, keepdims=True) + EPS)
    return (y * g) * (1.0 + scale) + shift


def _inproj_kernel(x_ref, g_ref, sh_ref, sc_ref, wm_ref, ws_ref, om_ref, oq_ref, os_ref, h_ref):
    j = pl.program_id(1)

    @pl.when(j == 0)
    def _():
        h = _norm_mod(x_ref[...], g_ref[...], sh_ref[...], sc_ref[...]).astype(jnp.bfloat16)
        h_ref[...] = h
        small = jnp.dot(h, ws_ref[...], preferred_element_type=jnp.float32)
        nq = IDX_HEADS * IDX_DIM
        oq_ref[...] = small[:, :nq].astype(jnp.bfloat16)
        os_ref[...] = small[:, nq:]

    om_ref[...] = jnp.dot(h_ref[...], wm_ref[...],
                          preferred_element_type=jnp.float32).astype(jnp.bfloat16)


def _inproj(x, g, shift, scale, w_main, w_small):
    s, d = x.shape
    n_main = w_main.shape[1]
    tm, tn = ROW_TILE, GROUP_WIDTH
    nq = IDX_HEADS * IDX_DIM
    row = lambda i, j: (0, 0)
    return pl.pallas_call(
        _inproj_kernel,
        grid=(s // tm, n_main // tn),
        in_specs=[pl.BlockSpec((tm, d), lambda i, j: (i, 0)),
                  pl.BlockSpec((1, d), row), pl.BlockSpec((1, d), row), pl.BlockSpec((1, d), row),
                  pl.BlockSpec((d, tn), lambda i, j: (0, j)),
                  pl.BlockSpec((d, SMALL_WIDTH), row)],
        out_specs=[pl.BlockSpec((tm, tn), lambda i, j: (i, j)),
                   pl.BlockSpec((tm, nq), lambda i, j: (i, 0)),
                   pl.BlockSpec((tm, SMALL_WIDTH - nq), lambda i, j: (i, 0))],
        out_shape=[jax.ShapeDtypeStruct((s, n_main), jnp.bfloat16),
                   jax.ShapeDtypeStruct((s, nq), jnp.bfloat16),
                   jax.ShapeDtypeStruct((s, SMALL_WIDTH - nq), jnp.float32)],
        scratch_shapes=[pltpu.VMEM((tm, d), jnp.bfloat16)],
        compiler_params=_cparams(("parallel", "arbitrary")),
        name="inproj",
    )(x, g, shift, scale, w_main, w_small)


def _fcum_kernel(fb_ref, b_ref, o_ref):
    r = fb_ref.shape[1]
    hi = lax.Precision.HIGHEST
    ci = lax.broadcasted_iota(jnp.int32, (LANES, LANES), 0)
    cj = lax.broadcasted_iota(jnp.int32, (LANES, LANES), 1)
    upper = (ci <= cj).astype(jnp.float32)
    ri = lax.broadcasted_iota(jnp.int32, (r, r), 0)
    rj = lax.broadcasted_iota(jnp.int32, (r, r), 1)
    strict_lower = (rj < ri).astype(jnp.float32)
    for h in range(N_HEADS):
        z = fb_ref[h] + b_ref[h]
        lf = jnp.minimum(z, 0.0) - jnp.log(1.0 + jnp.exp(-jnp.abs(z)))
        within = jnp.dot(lf, upper, precision=hi, preferred_element_type=jnp.float32)
        before = jnp.dot(strict_lower, within, precision=hi, preferred_element_type=jnp.float32)
        o_ref[h] = within + before[:, LANES - 1:LANES]


def _fcum(fb_t, b_forget):
    h, r, _ = fb_t.shape
    return pl.pallas_call(
        _fcum_kernel,
        in_specs=[pl.BlockSpec(memory_space=pltpu.VMEM),
                  pl.BlockSpec(memory_space=pltpu.SMEM)],
        out_specs=pl.BlockSpec(memory_space=pltpu.VMEM),
        out_shape=jax.ShapeDtypeStruct((h, r, LANES), jnp.float32),
        name="forget_cumsum",
    )(fb_t, b_forget)


def _key_to_float(key):
    bits = key ^ ((key >> 31) & jnp.int32(0x7FFFFFFF))
    return pltpu.bitcast(bits, jnp.float32)


def _select_kernel(qi_ref, kt_ref, sm_ref, mask_ref, s_ref, *, seq, n_sel):
    tq = qi_ref.shape[0]
    i = pl.program_id(0)
    n_chunks = i + 1
    n_rest = (seq - n_chunks * tq).astype(jnp.float32)
    row0 = i * tq
    wts = sm_ref[:, IDX_DIM:IDX_DIM + IDX_HEADS] * (IDX_HEADS ** -0.5)

    def score_chunk(c, _):
        col0 = pl.multiple_of(c * tq, tq)
        kc = kt_ref[:, pl.ds(col0, tq)]
        for r0 in range(0, tq, SEL_ROWS):
            acc = jnp.zeros((SEL_ROWS, tq), jnp.float32)
            for h in range(IDX_HEADS):
                lg = jnp.dot(qi_ref[r0:r0 + SEL_ROWS, h * IDX_DIM:(h + 1) * IDX_DIM], kc,
                             preferred_element_type=jnp.float32)
                acc = acc + jnp.maximum(lg, 0.0) * wts[r0:r0 + SEL_ROWS, h:h + 1]
            rows = row0 + r0 + lax.broadcasted_iota(jnp.int32, (SEL_ROWS, tq), 0)
            cols = col0 + lax.broadcasted_iota(jnp.int32, (SEL_ROWS, tq), 1)
            s_ref[r0:r0 + SEL_ROWS, pl.ds(col0, tq)] = jnp.where(cols <= rows, acc, NEG)
        return 0

    lax.fori_loop(0, n_chunks, score_chunk, 0)

    def sweep(r0, pred):
        def body(c, acc):
            col0 = pl.multiple_of(c * tq, tq)
            for k in range(tq // LANES):
                slab = s_ref[r0:r0 + SEL_ROWS, pl.ds(col0 + k * LANES, LANES)]
                acc = acc + jnp.where(pred(slab, col0 + k * LANES), 1.0, 0.0)
            return acc
        acc = lax.fori_loop(0, n_chunks, body, jnp.zeros((SEL_ROWS, LANES), jnp.float32))
        return jnp.sum(acc, axis=-1, keepdims=True)

    def lanes(col):
        return jnp.broadcast_to(col, (SEL_ROWS, LANES))

    lane_iota = lax.broadcasted_iota(jnp.int32, (SEL_ROWS, LANES), 1)
    n_idx_bits = max(1, (seq - 1).bit_length())
    for r0 in range(0, tq, SEL_ROWS):
        def bit_step(p, lo):
            cand = lo + lax.shift_left(jnp.int32(1), 31 - p)
            cf = _key_to_float(cand)
            cfb = lanes(cf)
            cnt = sweep(r0, lambda x, col0: x >= cfb) + jnp.where(NEG >= cf, n_rest, 0.0)
            return jnp.where(cnt >= n_sel, cand, lo)

        lo = lax.fori_loop(0, 32, bit_step, jnp.full((SEL_ROWS, 1), -2 ** 31, jnp.int32))
        thr = _key_to_float(lo)
        thrb = lanes(thr)

        cnt_gt = sweep(r0, lambda x, col0: x > thrb) + jnp.where(NEG > thr, n_rest, 0.0)
        cnt_eq = sweep(r0, lambda x, col0: x == thrb)
        need = n_sel - cnt_gt
        all_ties = jnp.full((SEL_ROWS, 1), seq - 1, jnp.int32)

        def tie_search(_):
            def idx_step(p, jlo):
                cand = jlo + lax.shift_left(jnp.int32(1), n_idx_bits - 1 - p)
                candb = lanes(cand)
                cnt = sweep(r0, lambda x, col0: (x == thrb) & (lane_iota + col0 < candb))
                return jnp.where(cnt < need, cand, jlo)
            return lax.fori_loop(0, n_idx_bits, idx_step, jnp.zeros((SEL_ROWS, 1), jnp.int32))

        excess = jnp.max(jnp.where(cnt_eq > need, 1.0, 0.0)) > 0.0
        jmax = lax.cond(excess, tie_search, lambda _: all_ties, 0)

        def mask_chunk(c, _):
            col0 = pl.multiple_of(c * tq, tq)
            x = s_ref[r0:r0 + SEL_ROWS, pl.ds(col0, tq)]
            rows = row0 + r0 + lax.broadcasted_iota(jnp.int32, x.shape, 0)
            cols = col0 + lax.broadcasted_iota(jnp.int32, x.shape, 1)
            keep = ((x > thr) | ((x == thr) & (cols <= jmax))) & (cols <= rows)
            mask_ref[r0:r0 + SEL_ROWS, pl.ds(col0, tq)] = jnp.where(keep, 0.0, NEG_MASK).astype(jnp.bfloat16)
            return 0

        lax.fori_loop(0, n_chunks, mask_chunk, 0)

    def fill_chunk(c, _):
        col0 = pl.multiple_of(c * tq, tq)
        mask_ref[:, pl.ds(col0, tq)] = jnp.full((tq, tq), NEG_MASK, jnp.bfloat16)
        return 0

    lax.fori_loop(n_chunks, seq // tq, fill_chunk, 0)


def _select(qi, k_t, small, n_sel):
    s = qi.shape[0]
    tq = ATT_TILE
    return pl.pallas_call(
        functools.partial(_select_kernel, seq=s, n_sel=n_sel),
        grid=(s // tq,),
        in_specs=[pl.BlockSpec((tq, qi.shape[1]), lambda i: (i, 0)),
                  pl.BlockSpec(k_t.shape, lambda i: (0, 0)),
                  pl.BlockSpec((tq, small.shape[1]), lambda i: (i, 0))],
        out_specs=pl.BlockSpec((tq, s), lambda i: (i, 0)),
        out_shape=jax.ShapeDtypeStruct((s, s), jnp.bfloat16),
        scratch_shapes=[pltpu.VMEM((tq, s), jnp.float32)],
        compiler_params=_cparams(("parallel",)),
        name="dsa_select",
    )(qi, k_t, small)


def _softmax_step(h, s, v, m_ref, acc_ref):
    ts = s.shape[1]
    m_prev = m_ref[h]
    m_new = jnp.maximum(m_prev, jnp.max(s, axis=-1, keepdims=True))
    alpha = jnp.exp2(m_prev - m_new)
    p = jnp.exp2(s - pltpu.repeat(m_new, ts // LANES, 1)).astype(jnp.bfloat16)
    v1 = jnp.concatenate([v, jnp.ones((ts, HEAD_DIM), jnp.bfloat16)], axis=1)
    acc_ref[h] = pltpu.repeat(alpha, 2 * HEAD_DIM // LANES, 1) * acc_ref[h] + jnp.dot(
        p, v1, preferred_element_type=jnp.float32)
    m_ref[h] = m_new


def _init_softmax(m_ref, acc_ref):
    m_ref[...] = jnp.full(m_ref.shape, NEG_MASK, jnp.float32)
    acc_ref[...] = jnp.zeros(acc_ref.shape, jnp.float32)


def _softmax_result(h, acc_ref):
    return acc_ref[h, :, :HEAD_DIM] / acc_ref[h, :, HEAD_DIM:]


def _qk(q, k):
    return lax.dot_general(q, k, (((1,), (1,)), ((), ())), preferred_element_type=jnp.float32)


def _dsa_kernel(qt_ref, st_ref, q_ref, k_ref, v_ref, mask_ref, bias_ref, o_ref, m_ref, acc_ref):
    p = pl.program_id(0)
    qi, si = qt_ref[p], st_ref[p]
    near = qi - si < bias_ref.shape[1]

    @pl.when(si == 0)
    def _():
        _init_softmax(m_ref, acc_ref)

    def sweep(with_bias):
        maskf = mask_ref[...].astype(jnp.float32)
        for h in range(N_HEADS):
            hs = slice(h * HEAD_DIM, (h + 1) * HEAD_DIM)
            s = _qk(q_ref[:, hs], k_ref[:, hs]) + maskf
            if with_bias:
                s = s + bias_ref[h, qi - si]
            _softmax_step(h, s, v_ref[:, hs], m_ref, acc_ref)

    pl.when(near)(functools.partial(sweep, True))
    pl.when(jnp.logical_not(near))(functools.partial(sweep, False))

    @pl.when(si == qi)
    def _():
        for h in range(N_HEADS):
            hs = slice(h * HEAD_DIM, (h + 1) * HEAD_DIM)
            o_ref[:, hs] = _softmax_result(h, acc_ref).astype(o_ref.dtype)


def _fox_kernel(qt_ref, st_ref, q_ref, k_ref, v_ref, g_ref, fq_ref, fs_ref, o_ref, m_ref, acc_ref):
    p = pl.program_id(0)
    qi, si = qt_ref[p], st_ref[p]
    tq, ts = q_ref.shape[0], k_ref.shape[0]

    @pl.when(si == 0)
    def _():
        _init_softmax(m_ref, acc_ref)

    def sweep(diagonal):
        if diagonal:
            causal = (lax.broadcasted_iota(jnp.int32, (tq, ts), 1)
                      <= lax.broadcasted_iota(jnp.int32, (tq, ts), 0))
        for h in range(N_HEADS):
            hs = slice(h * HEAD_DIM, (h + 1) * HEAD_DIM)
            decay = (fq_ref[h:h + 1, 0:1] - fs_ref[h:h + 1, :]) * LOG2E
            s = _qk(q_ref[:, hs], k_ref[:, hs]) + decay
            if diagonal:
                s = jnp.where(causal, s, NEG_MASK)
            _softmax_step(h, s, v_ref[:, hs], m_ref, acc_ref)

    pl.when(si == qi)(functools.partial(sweep, True))
    pl.when(si != qi)(functools.partial(sweep, False))

    @pl.when(si == qi)
    def _():
        for h in range(N_HEADS):
            hs = slice(h * HEAD_DIM, (h + 1) * HEAD_DIM)
            g = g_ref[:, hs].astype(jnp.float32)
            o_ref[:, hs] = (_softmax_result(h, acc_ref) / (1.0 + jnp.exp(-g))).astype(o_ref.dtype)


def _pair_tables(nq):
    qt = np.concatenate([np.full(q + 1, q, np.int32) for q in range(nq)])
    st = np.concatenate([np.arange(q + 1, dtype=np.int32) for q in range(nq)])
    return jnp.asarray(qt), jnp.asarray(st)


def _att_scratch(tq):
    return [pltpu.VMEM((N_HEADS, tq, LANES), jnp.float32),
            pltpu.VMEM((N_HEADS, tq, 2 * HEAD_DIM), jnp.float32)]


def _dsa(proj, mask, bias_tiles):
    s = proj.shape[0]
    t = ATT_TILE
    qt, st = _pair_tables(s // t)
    w = GROUP_WIDTH
    grid_spec = pltpu.PrefetchScalarGridSpec(
        num_scalar_prefetch=2,
        grid=(qt.shape[0],),
        in_specs=[pl.BlockSpec((t, w), lambda p, qt, st: (qt[p], 0)),
                  pl.BlockSpec((t, w), lambda p, qt, st: (st[p], 1)),
                  pl.BlockSpec((t, w), lambda p, qt, st: (st[p], 2)),
                  pl.BlockSpec((t, t), lambda p, qt, st: (qt[p], st[p])),
                  pl.BlockSpec(bias_tiles.shape, lambda p, qt, st: (0, 0, 0, 0))],
        out_specs=pl.BlockSpec((t, w), lambda p, qt, st: (qt[p], 0)),
        scratch_shapes=_att_scratch(t))
    return pl.pallas_call(
        _dsa_kernel, grid_spec=grid_spec,
        out_shape=jax.ShapeDtypeStruct((s, w), jnp.bfloat16),
        compiler_params=_cparams(("arbitrary",)),
        name="dsa_attention",
    )(qt, st, proj, proj, proj, mask, bias_tiles)


def _fox(proj, f_rows):
    s = proj.shape[0]
    t = FOX_TILE
    qt, st = _pair_tables(s // t)
    w = GROUP_WIDTH
    grid_spec = pltpu.PrefetchScalarGridSpec(
        num_scalar_prefetch=2,
        grid=(qt.shape[0],),
        in_specs=[pl.BlockSpec((t, w), lambda p, qt, st: (qt[p], 3)),
                  pl.BlockSpec((t, w), lambda p, qt, st: (st[p], 4)),
                  pl.BlockSpec((t, w), lambda p, qt, st: (st[p], 5)),
                  pl.BlockSpec((t, w), lambda p, qt, st: (qt[p], 6)),
                  pl.BlockSpec((N_HEADS, t), lambda p, qt, st: (0, qt[p])),
                  pl.BlockSpec((N_HEADS, t), lambda p, qt, st: (0, st[p]))],
        out_specs=pl.BlockSpec((t, w), lambda p, qt, st: (qt[p], 0)),
        scratch_shapes=_att_scratch(t))
    return pl.pallas_call(
        _fox_kernel, grid_spec=grid_spec,
        out_shape=jax.ShapeDtypeStruct((s, w), jnp.bfloat16),
        compiler_params=_cparams(("arbitrary",)),
        name="fox_attention",
    )(qt, st, proj, proj, proj, proj, f_rows, f_rows)


def _outproj_kernel(x_ref, oa_ref, ob_ref, wa_ref, wb_ref, gate_ref, o_ref):
    y = jnp.dot(oa_ref[...], wa_ref[...], preferred_element_type=jnp.float32)
    y = y + jnp.dot(ob_ref[...], wb_ref[...], preferred_element_type=jnp.float32)
    o_ref[...] = x_ref[...] + gate_ref[...] * y


def _outproj(x, o_a, o_b, w_a, w_b, gate):
    s, d = x.shape
    tm, tn = ROW_TILE, 1024
    w = GROUP_WIDTH
    return pl.pallas_call(
        _outproj_kernel,
        grid=(s // tm, d // tn),
        in_specs=[pl.BlockSpec((tm, tn), lambda i, j: (i, j)),
                  pl.BlockSpec((tm, w), lambda i, j: (i, 0)),
                  pl.BlockSpec((tm, w), lambda i, j: (i, 0)),
                  pl.BlockSpec((w, tn), lambda i, j: (0, j)),
                  pl.BlockSpec((w, tn), lambda i, j: (0, j)),
                  pl.BlockSpec((1, tn), lambda i, j: (0, j))],
        out_specs=pl.BlockSpec((tm, tn), lambda i, j: (i, j)),
        out_shape=jax.ShapeDtypeStruct((s, d), jnp.float32),
        compiler_params=_cparams(("parallel", "arbitrary")),
        name="outproj",
    )(x, o_a, o_b, w_a, w_b, gate)


def _conv(y_ref, cw_ref, cb_ref, tm):
    h = CONV_HALO
    return (cw_ref[2:3, :] * y_ref[h:h + tm, :] + cw_ref[1:2, :] * y_ref[h - 1:h - 1 + tm, :]
            + cw_ref[0:1, :] * y_ref[h - 2:h - 2 + tm, :] + cb_ref[...])


def _ffn_kernel(x_ref, xh_ref, g_ref, sh_ref, sc_ref, wg_ref, wv_ref, cwg_ref, cwv_ref, cbg_ref, cbv_ref,
                wd_ref, gate_ref, gf_ref, o_ref, h_ref, yg_ref, yv_ref, acc_ref):
    i, f = pl.program_id(0), pl.program_id(1)
    tm = x_ref.shape[0]

    @pl.when(f == 0)
    def _():
        halo = _norm_mod(xh_ref[...], g_ref[...], sh_ref[...], sc_ref[...])
        h_ref[0:CONV_HALO, :] = jnp.where(i > 0, halo, 0.0).astype(jnp.bfloat16)
        h_ref[CONV_HALO:, :] = _norm_mod(x_ref[...], g_ref[...], sh_ref[...], sc_ref[...]).astype(jnp.bfloat16)
        acc_ref[...] = jnp.zeros(acc_ref.shape, jnp.float32)

    hb = h_ref[...]
    yg_ref[...] = jnp.dot(hb, wg_ref[...], preferred_element_type=jnp.float32)
    yv_ref[...] = jnp.dot(hb, wv_ref[...], preferred_element_type=jnp.float32)
    ug = _conv(yg_ref, cwg_ref, cbg_ref, tm)
    uv = _conv(yv_ref, cwv_ref, cbv_ref, tm)
    a = (ug / (1.0 + jnp.exp(-ug))) * uv
    acc_ref[...] += jnp.dot(a.astype(jnp.bfloat16), wd_ref[...], preferred_element_type=jnp.float32)

    @pl.when(f == pl.num_programs(1) - 1)
    def _():
        x2 = x_ref[...] + gate_ref[...] * acc_ref[...]
        y = x2 * lax.rsqrt(jnp.mean(x2 * x2, axis=-1, keepdims=True) + EPS)
        o_ref[...] = y * gf_ref[...]


def _ffn(x, g, shift, scale, w_up, conv_w, conv_b, w_down, gate, g_final):
    s, d = x.shape
    fp = w_down.shape[0]
    tm, tf = ROW_TILE, FF_TILE
    nf = fp // tf
    hb = tm // CONV_HALO
    row = lambda i, f: (0, 0)
    return pl.pallas_call(
        _ffn_kernel,
        grid=(s // tm, nf),
        in_specs=[pl.BlockSpec((tm, d), lambda i, f: (i, 0)),
                  pl.BlockSpec((CONV_HALO, d), lambda i, f: (jnp.maximum(i * hb - 1, 0), 0)),
                  pl.BlockSpec((1, d), row), pl.BlockSpec((1, d), row), pl.BlockSpec((1, d), row),
                  pl.BlockSpec((d, tf), lambda i, f: (0, f)),
                  pl.BlockSpec((d, tf), lambda i, f: (0, f + nf)),
                  pl.BlockSpec((CONV_WIDTH, tf), lambda i, f: (0, f)),
                  pl.BlockSpec((CONV_WIDTH, tf), lambda i, f: (0, f + nf)),
                  pl.BlockSpec((1, tf), lambda i, f: (0, f)),
                  pl.BlockSpec((1, tf), lambda i, f: (0, f + nf)),
                  pl.BlockSpec((tf, d), lambda i, f: (f, 0)),
                  pl.BlockSpec((1, d), row), pl.BlockSpec((1, d), row)],
        out_specs=pl.BlockSpec((tm, d), lambda i, f: (i, 0)),
        out_shape=jax.ShapeDtypeStruct((s, d), jnp.float32),
        scratch_shapes=[pltpu.VMEM((tm + CONV_HALO, d), jnp.bfloat16),
                        pltpu.VMEM((tm + CONV_HALO, tf), jnp.float32),
                        pltpu.VMEM((tm + CONV_HALO, tf), jnp.float32),
                        pltpu.VMEM((tm, d), jnp.float32)],
        compiler_params=_cparams(("parallel", "arbitrary")),
        name="conv_ffn",
    )(x, x, g, shift, scale, w_up, w_up, conv_w, conv_w, conv_b, conv_b, w_down, gate, g_final)


def _t5_bias_tiles(rel_bias, t):
    max_exact = N_BUCKETS // 2
    d = jnp.arange(2 * t, dtype=jnp.int32)
    df = jnp.maximum(d, 1).astype(jnp.float32)
    large = max_exact + (jnp.log(df / max_exact) / math.log(MAX_DISTANCE / max_exact)
                         * (N_BUCKETS - max_exact)).astype(jnp.int32)
    bucket = jnp.where(d < max_exact, d, jnp.minimum(large, N_BUCKETS - 1))
    rb = rel_bias.astype(jnp.float32)
    table = (rb[bucket] - rb[N_BUCKETS - 1]) * LOG2E
    period = 2 * t
    k = np.arange(period)
    tiles = []
    for off in (0, t):
        idx = np.clip(np.where(k < t, off - k, off + period - k), 0, 2 * t - 1)
        seq = jnp.tile(table[idx].T, (1, t))
        tiles.append(seq[:, :t * (period - 1)].reshape(N_HEADS, t, period - 1)[:, :, :t])
    return jnp.stack(tiles, axis=1)


def _pad_cols(a, n):
    return jnp.pad(a, ((0, 0), (0, n - a.shape[1])))


def kernel(x, c, rel_bias, w_ada, b_ada, g_attn, w_in, b_forget, w_out, g_mlp, w_up, conv_w, conv_b,
           w_down, g_final):
    b, s, d = x.shape
    assert b == 1 and s % ROW_TILE == 0 and s % ATT_TILE == 0 and s % FOX_TILE == 0
    assert ATT_TILE >= MAX_DISTANCE
    depth = w_ada.shape[0]
    assert depth == 1, "the fused MLP kernel applies the final norm, so it must be the last layer"
    d_ff = w_down.shape[1]
    ff_pad = -(-d_ff // FF_TILE) * FF_TILE
    n_sel = min(TOPK_MAX, s // 4)
    gw, nq = GROUP_WIDTH, IDX_HEADS * IDX_DIM
    bf = jnp.bfloat16
    xs = x[0]
    c_col = c.reshape(d, 1)
    bias_tiles = _t5_bias_tiles(rel_bias, ATT_TILE)

    for l in range(depth):
        mod = _ada(c_col, w_ada[l], b_ada[l][None, :])
        shift_a, scale_a, gate_a, shift_m, scale_m, gate_m = [mod[:, k * d:(k + 1) * d] for k in range(6)]

        wi = w_in[l]
        o = np.cumsum([0, gw, gw, gw, nq, IDX_DIM, IDX_HEADS, gw, gw, gw, gw, N_HEADS])
        seg = lambda k: wi[:, o[k]:o[k + 1]]
        q_scale = HEAD_DIM ** -0.5 * LOG2E
        w_main = jnp.concatenate([seg(0) * q_scale, seg(1), seg(2),
                                  seg(6) * q_scale, seg(7), seg(8), seg(9)], axis=1).astype(bf)
        w_small = _pad_cols(jnp.concatenate([seg(3) * IDX_DIM ** -0.5, seg(4), seg(5), seg(10)], axis=1),
                            SMALL_WIDTH).astype(bf)

        proj, qi, small = _inproj(xs, g_attn[l][None, :], shift_a, scale_a, w_main, w_small)

        k_t = small[:, :IDX_DIM].T.astype(bf)
        fb_t = small[:, IDX_DIM + IDX_HEADS:IDX_DIM + IDX_HEADS + N_HEADS].T
        f_rows = _fcum(fb_t.reshape(N_HEADS, s // LANES, LANES), b_forget[l]).reshape(N_HEADS, s)

        mask = _select(qi, k_t, small, n_sel)
        o_a = _dsa(proj, mask, bias_tiles)
        o_b = _fox(proj, f_rows)

        wo = w_out[l].astype(bf)
        x1 = _outproj(xs, o_a, o_b, wo[:gw], wo[gw:], gate_a)

        wu = w_up[l]
        w_up_p = jnp.concatenate([_pad_cols(wu[:, :d_ff], ff_pad), _pad_cols(wu[:, d_ff:], ff_pad)], axis=1).astype(bf)
        cw = conv_w[l]
        cw_p = jnp.concatenate([_pad_cols(cw[:, :d_ff], ff_pad), _pad_cols(cw[:, d_ff:], ff_pad)], axis=1)
        cb = conv_b[l][None, :]
        cb_p = jnp.concatenate([_pad_cols(cb[:, :d_ff], ff_pad), _pad_cols(cb[:, d_ff:], ff_pad)], axis=1)
        w_down_p = jnp.pad(w_down[l], ((0, ff_pad - d_ff), (0, 0))).astype(bf)

        xs = _ffn(x1, g_mlp[l][None, :], shift_m, scale_m, w_up_p, cw_p, cb_p, w_down_p, gate_m,
                  g_final[None, :])

    return xs[None]
```

```python
import functools
import math

import jax
import jax.numpy as jnp
import numpy as np
from jax import lax
from jax.experimental import pallas as pl
from jax.experimental.pallas import tpu as pltpu

HEAD_DIM = 128
N_HEADS = 8
GROUP_WIDTH = N_HEADS * HEAD_DIM
IDX_HEADS = 8
IDX_DIM = 64
TOPK_MAX = 256
N_BUCKETS = 32
MAX_DISTANCE = 128
CONV_WIDTH = 3
EPS = 1e-6
NEG = -1e30
NEG_MASK = -(2.0 ** 100)

LANES = 128
SUBLANES = 8
VMEM_LIMIT = 56 * 1024 * 1024

ROW_TILE = 512
ATT_TILE = 256
FOX_TILE = 512
LOG2E = math.log2(math.e)
SEL_ROWS = 128
SEL_CHUNK = 512
COUNT_ROWS = 64
RANK_TILE = 256
FF_TILE = 512
CONV_HALO = 16
SMALL_WIDTH = 640


def _cparams(sem):
    return pltpu.CompilerParams(dimension_semantics=sem, vmem_limit_bytes=VMEM_LIMIT)


def _ada_kernel(c_ref, w_ref, b_ref, o_ref):
    d = w_ref.shape[0]
    rows = 256

    def body(r, acc):
        sl = pl.ds(pl.multiple_of(r * rows, rows), rows)
        cc = c_ref[sl, :]
        ca = cc / (1.0 + jnp.exp(-cc))
        prod = w_ref[sl, :] * ca
        return acc + prod.reshape(rows // SUBLANES, SUBLANES, -1).sum(axis=0)

    acc = lax.fori_loop(0, d // rows, body, jnp.zeros((SUBLANES, w_ref.shape[1]), jnp.float32))
    o_ref[...] = jnp.sum(acc, axis=0, keepdims=True) + b_ref[...]


def _ada(c_col, w, b_row):
    d, n = w.shape
    tn = 1024
    return pl.pallas_call(
        _ada_kernel,
        grid=(n // tn,),
        in_specs=[pl.BlockSpec((d, 1), lambda j: (0, 0)),
                  pl.BlockSpec((d, tn), lambda j: (0, j)),
                  pl.BlockSpec((1, tn), lambda j: (0, j))],
        out_specs=pl.BlockSpec((1, tn), lambda j: (0, j)),
        out_shape=jax.ShapeDtypeStruct((1, n), jnp.float32),
        compiler_params=_cparams(("arbitrary",)),
        name="adaln",
    )(c_col, w, b_row)


def _norm_mod(x, g, shift, scale):
    y = x * lax.rsqrt(jnp.mean(x * x, axis=-1, keepdims=True) + EPS)
    return (y * g) * (1.0 + scale) + shift


def _inproj_kernel(x_ref, g_ref, sh_ref, sc_ref, wm_ref, ws_ref, om_ref, oq_ref, os_ref, h_ref):
    j = pl.program_id(1)

    @pl.when(j == 0)
    def _():
        h = _norm_mod(x_ref[...], g_ref[...], sh_ref[...], sc_ref[...]).astype(jnp.bfloat16)
        h_ref[...] = h
        small = jnp.dot(h, ws_ref[...], preferred_element_type=jnp.float32)
        nq = IDX_HEADS * IDX_DIM
        oq_ref[...] = small[:, :nq].astype(jnp.bfloat16)
        os_ref[...] = small[:, nq:]

    om_ref[...] = jnp.dot(h_ref[...], wm_ref[...],
                          preferred_element_type=jnp.float32).astype(jnp.bfloat16)


def _inproj(x, g, shift, scale, w_main, w_small):
    s, d = x.shape
    n_main = w_main.shape[1]
    tm, tn = ROW_TILE, GROUP_WIDTH
    nq = IDX_HEADS * IDX_DIM
    row = lambda i, j: (0, 0)
    return pl.pallas_call(
        _inproj_kernel,
        grid=(s // tm, n_main // tn),
        in_specs=[pl.BlockSpec((tm, d), lambda i, j: (i, 0)),
                  pl.BlockSpec((1, d), row), pl.BlockSpec((1, d), row), pl.BlockSpec((1, d), row),
                  pl.BlockSpec((d, tn), lambda i, j: (0, j)),
                  pl.BlockSpec((d, SMALL_WIDTH), row)],
        out_specs=[pl.BlockSpec((tm, tn), lambda i, j: (i, j)),
                   pl.BlockSpec((tm, nq), lambda i, j: (i, 0)),
                   pl.BlockSpec((tm, SMALL_WIDTH - nq), lambda i, j: (i, 0))],
        out_shape=[jax.ShapeDtypeStruct((s, n_main), jnp.bfloat16),
                   jax.ShapeDtypeStruct((s, nq), jnp.bfloat16),
                   jax.ShapeDtypeStruct((s, SMALL_WIDTH - nq), jnp.float32)],
        scratch_shapes=[pltpu.VMEM((tm, d), jnp.bfloat16)],
        compiler_params=_cparams(("parallel", "arbitrary")),
        name="inproj",
    )(x, g, shift, scale, w_main, w_small)


def _fcum_kernel(fb_ref, b_ref, o_ref):
    r = fb_ref.shape[1]
    hi = lax.Precision.HIGHEST
    ci = lax.broadcasted_iota(jnp.int32, (LANES, LANES), 0)
    cj = lax.broadcasted_iota(jnp.int32, (LANES, LANES), 1)
    upper = (ci <= cj).astype(jnp.float32)
    ri = lax.broadcasted_iota(jnp.int32, (r, r), 0)
    rj = lax.broadcasted_iota(jnp.int32, (r, r), 1)
    strict_lower = (rj < ri).astype(jnp.float32)
    for h in range(N_HEADS):
        z = fb_ref[h] + b_ref[h]
        lf = jnp.minimum(z, 0.0) - jnp.log(1.0 + jnp.exp(-jnp.abs(z)))
        within = jnp.dot(lf, upper, precision=hi, preferred_element_type=jnp.float32)
        before = jnp.dot(strict_lower, within, precision=hi, preferred_element_type=jnp.float32)
        o_ref[h] = within + before[:, LANES - 1:LANES]


def _fcum(fb_t, b_forget):
    h, r, _ = fb_t.shape
    return pl.pallas_call(
        _fcum_kernel,
        in_specs=[pl.BlockSpec(memory_space=pltpu.VMEM),
                  pl.BlockSpec(memory_space=pltpu.SMEM)],
        out_specs=pl.BlockSpec(memory_space=pltpu.VMEM),
        out_shape=jax.ShapeDtypeStruct((h, r, LANES), jnp.float32),
        name="forget_cumsum",
    )(fb_t, b_forget)


def _ordered_bits(v):
    return v ^ ((v >> 31) & jnp.int32(0x7FFFFFFF))


def _float_to_key(x):
    return _ordered_bits(pltpu.bitcast(x, jnp.int32))


def _key_to_float(key):
    return pltpu.bitcast(_ordered_bits(key), jnp.float32)


def _fold_rows(x, op):
    parts = [x[r:r + SUBLANES] for r in range(0, x.shape[0], SUBLANES)]
    while len(parts) > 1:
        parts = [op(parts[k], parts[k + 1]) for k in range(0, len(parts) - 1, 2)] + (
            [parts[-1]] if len(parts) % 2 else [])
    return parts[0]


def _select_kernel(k_ref, qit_ref, wt_ref, mask_ref, s_ref, top_ref, *, seq, n_sel):
    tq = qit_ref.shape[1]
    ck = SEL_CHUNK
    i = pl.program_id(0)
    n_chunks = ((i + 1) * tq + ck - 1) // ck
    n_rest = (seq - n_chunks * ck).astype(jnp.float32)
    q_pos = i * tq + lax.broadcasted_iota(jnp.int32, (SEL_ROWS, tq), 1)
    key_iota = lax.broadcasted_iota(jnp.int32, (SEL_ROWS, tq), 0)
    wts = wt_ref[...] * (IDX_HEADS ** -0.5)
    n_slots = top_ref.shape[0]

    top_ref[...] = jnp.full(top_ref.shape, -jnp.inf, jnp.float32)

    def score_chunk(c, _):
        for r in range(0, ck, SEL_ROWS):
            r0 = pl.multiple_of(c * ck + r, SEL_ROWS)
            kc = k_ref[pl.ds(r0, SEL_ROWS), :]
            acc = jnp.zeros((SEL_ROWS, tq), jnp.float32)
            for h in range(IDX_HEADS):
                lg = jnp.dot(kc, qit_ref[h * IDX_DIM:(h + 1) * IDX_DIM, :],
                             preferred_element_type=jnp.float32)
                acc = acc + jnp.maximum(lg, 0.0) * wts[h:h + 1, :]
            sc = jnp.where(r0 + key_iota <= q_pos, acc, NEG)
            s_ref[pl.ds(r0, SEL_ROWS), :] = sc
            slot = r % n_slots
            top_ref[slot:slot + SEL_ROWS, :] = jnp.maximum(top_ref[slot:slot + SEL_ROWS, :], sc)
        return 0

    lax.fori_loop(0, n_chunks, score_chunk, 0)

    def count(pred):
        def body(c, accs):
            for r in range(0, ck, COUNT_ROWS):
                x = s_ref[pl.ds(pl.multiple_of(c * ck + r, COUNT_ROWS), COUNT_ROWS), :]
                accs = tuple(a + _fold_rows(jnp.where(f(x), 1.0, 0.0), jnp.add) for a, f in zip(accs, pred))
            return accs
        zero = jnp.zeros((SUBLANES, tq), jnp.float32)
        accs = lax.fori_loop(0, n_chunks, body, (zero,) * len(pred))
        return [jnp.sum(a, axis=0, keepdims=True) for a in accs]

    def count_ge(cf):
        return count([lambda x: x >= cf])[0] + jnp.where(NEG >= cf, n_rest, 0.0)

    top = top_ref[...]
    low_key = _float_to_key(jnp.min(_fold_rows(top, jnp.minimum), axis=0, keepdims=True))
    max_key = _float_to_key(jnp.max(_fold_rows(top, jnp.maximum), axis=0, keepdims=True))
    hi0 = jnp.maximum(max_key, _float_to_key(jnp.full((1, tq), NEG, jnp.float32))) + 1
    lo0 = jnp.full((1, tq), -2 ** 31, jnp.int32)
    max_passes = 96

    n_pos, n_nonneg = count([lambda x: x > 0.0, lambda x: x >= 0.0])
    at_zero = jnp.logical_and(n_pos < n_sel, n_nonneg >= n_sel)
    above_zero = n_pos >= n_sel
    below_zero = n_nonneg < n_sel
    zero_key = jnp.zeros((1, tq), jnp.int32)
    lo0 = jnp.where(below_zero, lo0, zero_key)
    c_lo0 = jnp.where(below_zero, float(seq), n_nonneg)
    hi0 = jnp.where(above_zero, hi0, jnp.where(at_zero, zero_key + 2 ** 23, zero_key - 1))
    c_hi0 = jnp.where(above_zero, 0.0, jnp.where(at_zero, n_pos, n_nonneg))
    done0 = jnp.where(at_zero, 1, 0)

    def searching(st):
        p, _, _, _, _, done = st
        return jnp.logical_and(p < max_passes, jnp.min(done) == 0)

    def search_pass(st):
        p, lo, hi, c_lo, c_hi, done = st
        mid = (lo >> 1) + (hi >> 1) + (lo & hi & 1)
        span = (hi - lo).astype(jnp.float32)
        frac = (c_lo - (n_sel - 0.5)) / (c_lo - c_hi)
        interp = lo + jnp.minimum(frac * span, 2.0 ** 31 - 256.0).astype(jnp.int32)
        use_interp = jnp.logical_and((lo ^ hi) >= 0, (p & 1) == 1)
        cand = jnp.where(p == 0, low_key, jnp.where(use_interp, interp, mid))
        cand = jnp.minimum(jnp.maximum(cand, lo + 1), hi - 1)
        cnt = count_ge(_key_to_float(cand))
        active = done == 0
        up = jnp.logical_and(active, cnt >= n_sel)
        down = jnp.logical_and(active, cnt < n_sel)
        lo, c_lo = jnp.where(up, cand, lo), jnp.where(up, cnt, c_lo)
        hi, c_hi = jnp.where(down, cand, hi), jnp.where(down, cnt, c_hi)
        finished = jnp.logical_and(active, jnp.logical_or(cnt == n_sel, hi - 1 <= lo))
        return p + 1, lo, hi, c_lo, c_hi, jnp.where(finished, 1, done)

    init = (jnp.int32(0), lo0, hi0, c_lo0, c_hi0, done0)
    _, lo, _, c_lo, c_hi, _ = lax.while_loop(searching, search_pass, init)
    thr = _key_to_float(lo)

    need = jnp.where(c_lo > n_sel, n_sel - c_hi, float(seq))
    rt = RANK_TILE
    earlier = (lax.broadcasted_iota(jnp.int32, (rt, rt), 1)
               < lax.broadcasted_iota(jnp.int32, (rt, rt), 0)).astype(jnp.bfloat16)

    def mask_chunk(c, ties_before):
        for t0 in range(0, ck, rt):
            c0 = pl.multiple_of(c * ck + t0, rt)
            tie = jnp.where(s_ref[pl.ds(c0, rt), :] == thr, 1.0, 0.0)
            rank = ties_before + jnp.dot(earlier, tie.astype(jnp.bfloat16),
                                         preferred_element_type=jnp.float32)
            for r in range(0, rt, SEL_ROWS):
                r0 = pl.multiple_of(c0 + r, SEL_ROWS)
                x = s_ref[pl.ds(r0, SEL_ROWS), :]
                kept_tie = jnp.logical_and(x == thr, rank[r:r + SEL_ROWS] < need)
                keep = jnp.logical_and(jnp.logical_or(x > thr, kept_tie), r0 + key_iota <= q_pos)
                mask_ref[:, pl.ds(r0, SEL_ROWS)] = jnp.where(keep, 0.0, NEG_MASK).T.astype(jnp.bfloat16)
            ties_before = ties_before + jnp.sum(_fold_rows(tie, jnp.add), axis=0, keepdims=True)
        return ties_before

    lax.fori_loop(0, n_chunks, mask_chunk, jnp.zeros((1, tq), jnp.float32))

    def fill_chunk(c, _):
        r0 = pl.multiple_of(c * ck, ck)
        mask_ref[:, pl.ds(r0, ck)] = jnp.full((tq, ck), NEG_MASK, jnp.bfloat16)
        return 0

    lax.fori_loop(n_chunks, seq // ck, fill_chunk, 0)


def _select(k_idx, qi_t, w_t, n_sel):
    s = k_idx.shape[0]
    tq = ATT_TILE
    assert n_sel <= RANK_TILE and s % SEL_CHUNK == 0 and SEL_CHUNK % RANK_TILE == 0
    return pl.pallas_call(
        functools.partial(_select_kernel, seq=s, n_sel=n_sel),
        grid=(s // tq,),
        in_specs=[pl.BlockSpec(k_idx.shape, lambda i: (0, 0)),
                  pl.BlockSpec((qi_t.shape[0], tq), lambda i: (0, i)),
                  pl.BlockSpec((IDX_HEADS, tq), lambda i: (0, i))],
        out_specs=pl.BlockSpec((tq, s), lambda i: (i, 0)),
        out_shape=jax.ShapeDtypeStruct((s, s), jnp.bfloat16),
        scratch_shapes=[pltpu.VMEM((s, tq), jnp.float32), pltpu.VMEM((RANK_TILE, tq), jnp.float32)],
        compiler_params=_cparams(("parallel",)),
        name="dsa_select",
    )(k_idx, qi_t, w_t)


def _qk(a, b):
    return lax.dot_general(a, b, (((1,), (1,)), ((), ())), preferred_element_type=jnp.float32)


def _pair_tables(nq):
    qt = np.concatenate([np.full(q + 1, q, np.int32) for q in range(nq)])
    st = np.concatenate([np.arange(q + 1, dtype=np.int32) for q in range(nq)])
    return jnp.asarray(qt), jnp.asarray(st)


def _softmax_step(h, s, v, m_ref, acc_ref):
    ts = s.shape[1]
    m_prev = m_ref[h]
    m_new = jnp.maximum(m_prev, jnp.max(s, axis=-1, keepdims=True))
    alpha = jnp.exp2(m_prev - m_new)
    pm = jnp.exp2(s - jnp.tile(m_new, (1, ts // LANES))).astype(jnp.bfloat16)
    v1 = jnp.concatenate([v, jnp.ones((ts, HEAD_DIM), jnp.bfloat16)], axis=1)
    acc_ref[h] = (jnp.tile(alpha, (1, 2 * HEAD_DIM // LANES)) * acc_ref[h]
                  + jnp.dot(pm, v1, preferred_element_type=jnp.float32))
    m_ref[h] = m_new


def _init_softmax(m_ref, acc_ref):
    m_ref[...] = jnp.full(m_ref.shape, NEG_MASK, jnp.float32)
    acc_ref[...] = jnp.zeros(acc_ref.shape, jnp.float32)


def _softmax_result(h, acc_ref):
    return acc_ref[h, :, :HEAD_DIM] / acc_ref[h, :, HEAD_DIM:]


def _att_scratch(tq):
    return [pltpu.VMEM((N_HEADS, tq, LANES), jnp.float32),
            pltpu.VMEM((N_HEADS, tq, 2 * HEAD_DIM), jnp.float32)]


def _dsa_kernel(qt_ref, st_ref, q_ref, k_ref, v_ref, mask_ref, bias_ref, o_ref, m_ref, acc_ref):
    p = pl.program_id(0)
    qi, si = qt_ref[p], st_ref[p]
    near = qi - si < bias_ref.shape[1]

    @pl.when(si == 0)
    def _():
        _init_softmax(m_ref, acc_ref)

    def sweep(with_bias):
        maskf = mask_ref[...].astype(jnp.float32)
        for h in range(N_HEADS):
            hs = slice(h * HEAD_DIM, (h + 1) * HEAD_DIM)
            s = _qk(q_ref[:, hs], k_ref[:, hs]) + maskf
            if with_bias:
                s = s + bias_ref[h, qi - si]
            _softmax_step(h, s, v_ref[:, hs], m_ref, acc_ref)

    pl.when(near)(functools.partial(sweep, True))
    pl.when(jnp.logical_not(near))(functools.partial(sweep, False))

    @pl.when(si == qi)
    def _():
        for h in range(N_HEADS):
            hs = slice(h * HEAD_DIM, (h + 1) * HEAD_DIM)
            o_ref[:, hs] = _softmax_result(h, acc_ref).astype(o_ref.dtype)


def _dsa(proj, mask, bias_tiles):
    s = proj.shape[0]
    t = ATT_TILE
    qt, st = _pair_tables(s // t)
    w = GROUP_WIDTH
    grid_spec = pltpu.PrefetchScalarGridSpec(
        num_scalar_prefetch=2,
        grid=(qt.shape[0],),
        in_specs=[pl.BlockSpec((t, w), lambda p, qt, st: (qt[p], 0)),
                  pl.BlockSpec((t, w), lambda p, qt, st: (st[p], 1)),
                  pl.BlockSpec((t, w), lambda p, qt, st: (st[p], 2)),
                  pl.BlockSpec((t, t), lambda p, qt, st: (qt[p], st[p])),
                  pl.BlockSpec(bias_tiles.shape, lambda p, qt, st: (0, 0, 0, 0))],
        out_specs=pl.BlockSpec((t, w), lambda p, qt, st: (qt[p], 0)),
        scratch_shapes=_att_scratch(t))
    return pl.pallas_call(
        _dsa_kernel, grid_spec=grid_spec,
        out_shape=jax.ShapeDtypeStruct((s, w), jnp.bfloat16),
        compiler_params=_cparams(("arbitrary",)),
        name="dsa_attention",
    )(qt, st, proj, proj, proj, mask, bias_tiles)


def _fox_kernel(qt_ref, st_ref, q_ref, k_ref, v_ref, g_ref, fq_ref, fs_ref, o_ref, m_ref, acc_ref):
    p = pl.program_id(0)
    qi, si = qt_ref[p], st_ref[p]
    tq, ts = q_ref.shape[0], k_ref.shape[0]

    @pl.when(si == 0)
    def _():
        _init_softmax(m_ref, acc_ref)

    def sweep(diagonal):
        if diagonal:
            causal = (lax.broadcasted_iota(jnp.int32, (tq, ts), 1)
                      <= lax.broadcasted_iota(jnp.int32, (tq, ts), 0))
        for h in range(N_HEADS):
            hs = slice(h * HEAD_DIM, (h + 1) * HEAD_DIM)
            decay = (fq_ref[h:h + 1, 0:1] - fs_ref[h:h + 1, :]) * LOG2E
            s = _qk(q_ref[:, hs], k_ref[:, hs]) + decay
            if diagonal:
                s = jnp.where(causal, s, NEG_MASK)
            _softmax_step(h, s, v_ref[:, hs], m_ref, acc_ref)

    pl.when(si == qi)(functools.partial(sweep, True))
    pl.when(si != qi)(functools.partial(sweep, False))

    @pl.when(si == qi)
    def _():
        for h in range(N_HEADS):
            hs = slice(h * HEAD_DIM, (h + 1) * HEAD_DIM)
            g = g_ref[:, hs].astype(jnp.float32)
            o_ref[:, hs] = (_softmax_result(h, acc_ref) / (1.0 + jnp.exp(-g))).astype(o_ref.dtype)


def _fox(proj, f_rows):
    s = proj.shape[0]
    t = FOX_TILE
    qt, st = _pair_tables(s // t)
    w = GROUP_WIDTH
    grid_spec = pltpu.PrefetchScalarGridSpec(
        num_scalar_prefetch=2,
        grid=(qt.shape[0],),
        in_specs=[pl.BlockSpec((t, w), lambda p, qt, st: (qt[p], 3)),
                  pl.BlockSpec((t, w), lambda p, qt, st: (st[p], 4)),
                  pl.BlockSpec((t, w), lambda p, qt, st: (st[p], 5)),
                  pl.BlockSpec((t, w), lambda p, qt, st: (qt[p], 6)),
                  pl.BlockSpec((N_HEADS, t), lambda p, qt, st: (0, qt[p])),
                  pl.BlockSpec((N_HEADS, t), lambda p, qt, st: (0, st[p]))],
        out_specs=pl.BlockSpec((t, w), lambda p, qt, st: (qt[p], 0)),
        scratch_shapes=_att_scratch(t))
    return pl.pallas_call(
        _fox_kernel, grid_spec=grid_spec,
        out_shape=jax.ShapeDtypeStruct((s, w), jnp.bfloat16),
        compiler_params=_cparams(("arbitrary",)),
        name="fox_attention",
    )(qt, st, proj, proj, proj, proj, f_rows, f_rows)


def _outproj_kernel(x_ref, oa_ref, ob_ref, wa_ref, wb_ref, gate_ref, o_ref):
    y = jnp.dot(oa_ref[...], wa_ref[...], preferred_element_type=jnp.float32)
    y = y + jnp.dot(ob_ref[...], wb_ref[...], preferred_element_type=jnp.float32)
    o_ref[...] = x_ref[...] + gate_ref[...] * y


def _outproj(x, o_a, o_b, w_a, w_b, gate):
    s, d = x.shape
    tm, tn = ROW_TILE, 1024
    w = GROUP_WIDTH
    return pl.pallas_call(
        _outproj_kernel,
        grid=(s // tm, d // tn),
        in_specs=[pl.BlockSpec((tm, tn), lambda i, j: (i, j)),
                  pl.BlockSpec((tm, w), lambda i, j: (i, 0)),
                  pl.BlockSpec((tm, w), lambda i, j: (i, 0)),
                  pl.BlockSpec((w, tn), lambda i, j: (0, j)),
                  pl.BlockSpec((w, tn), lambda i, j: (0, j)),
                  pl.BlockSpec((1, tn), lambda i, j: (0, j))],
        out_specs=pl.BlockSpec((tm, tn), lambda i, j: (i, j)),
        out_shape=jax.ShapeDtypeStruct((s, d), jnp.float32),
        compiler_params=_cparams(("parallel", "arbitrary")),
        name="outproj",
    )(x, o_a, o_b, w_a, w_b, gate)


def _conv(y_ref, cw_ref, cb_ref, tm):
    h = CONV_HALO
    return (cw_ref[2:3, :] * y_ref[h:h + tm, :] + cw_ref[1:2, :] * y_ref[h - 1:h - 1 + tm, :]
            + cw_ref[0:1, :] * y_ref[h - 2:h - 2 + tm, :] + cb_ref[...])


def _ffn_kernel(x_ref, xh_ref, g_ref, sh_ref, sc_ref, wg_ref, wv_ref, cwg_ref, cwv_ref, cbg_ref, cbv_ref,
                wd_ref, gate_ref, gf_ref, o_ref, h_ref, yg_ref, yv_ref, acc_ref):
    i, f = pl.program_id(0), pl.program_id(1)
    tm = x_ref.shape[0]

    @pl.when(f == 0)
    def _():
        halo = _norm_mod(xh_ref[...], g_ref[...], sh_ref[...], sc_ref[...])
        h_ref[0:CONV_HALO, :] = jnp.where(i > 0, halo, 0.0).astype(jnp.bfloat16)
        h_ref[CONV_HALO:, :] = _norm_mod(x_ref[...], g_ref[...], sh_ref[...], sc_ref[...]).astype(jnp.bfloat16)
        acc_ref[...] = jnp.zeros(acc_ref.shape, jnp.float32)

    hb = h_ref[...]
    yg_ref[...] = jnp.dot(hb, wg_ref[...], preferred_element_type=jnp.float32)
    yv_ref[...] = jnp.dot(hb, wv_ref[...], preferred_element_type=jnp.float32)
    ug = _conv(yg_ref, cwg_ref, cbg_ref, tm)
    uv = _conv(yv_ref, cwv_ref, cbv_ref, tm)
    a = (ug / (1.0 + jnp.exp(-ug))) * uv
    acc_ref[...] += jnp.dot(a.astype(jnp.bfloat16), wd_ref[...], preferred_element_type=jnp.float32)

    @pl.when(f == pl.num_programs(1) - 1)
    def _():
        x2 = x_ref[...] + gate_ref[...] * acc_ref[...]
        y = x2 * lax.rsqrt(jnp.mean(x2 * x2, axis=-1, keepdims=True) + EPS)
        o_ref[...] = y * gf_ref[...]


def _ffn(x, g, shift, scale, w_up, conv_w, conv_b, w_down, gate, g_final):
    s, d = x.shape
    fp = w_down.shape[0]
    tm, tf = ROW_TILE, FF_TILE
    nf = fp // tf
    hb = tm // CONV_HALO
    row = lambda i, f: (0, 0)
    return pl.pallas_call(
        _ffn_kernel,
        grid=(s // tm, nf),
        in_specs=[pl.BlockSpec((tm, d), lambda i, f: (i, 0)),
                  pl.BlockSpec((CONV_HALO, d), lambda i, f: (jnp.maximum(i * hb - 1, 0), 0)),
                  pl.BlockSpec((1, d), row), pl.BlockSpec((1, d), row), pl.BlockSpec((1, d), row),
                  pl.BlockSpec((d, tf), lambda i, f: (0, f)),
                  pl.BlockSpec((d, tf), lambda i, f: (0, f + nf)),
                  pl.BlockSpec((CONV_WIDTH, tf), lambda i, f: (0, f)),
                  pl.BlockSpec((CONV_WIDTH, tf), lambda i, f: (0, f + nf)),
                  pl.BlockSpec((1, tf), lambda i, f: (0, f)),
                  pl.BlockSpec((1, tf), lambda i, f: (0, f + nf)),
                  pl.BlockSpec((tf, d), lambda i, f: (f, 0)),
                  pl.BlockSpec((1, d), row), pl.BlockSpec((1, d), row)],
        out_specs=pl.BlockSpec((tm, d), lambda i, f: (i, 0)),
        out_shape=jax.ShapeDtypeStruct((s, d), jnp.float32),
        scratch_shapes=[pltpu.VMEM((tm + CONV_HALO, d), jnp.bfloat16),
                        pltpu.VMEM((tm + CONV_HALO, tf), jnp.float32),
                        pltpu.VMEM((tm + CONV_HALO, tf), jnp.float32),
                        pltpu.VMEM((tm, d), jnp.float32)],
        compiler_params=_cparams(("parallel", "arbitrary")),
        name="conv_ffn",
    )(x, x, g, shift, scale, w_up, w_up, conv_w, conv_w, conv_b, conv_b, w_down, gate, g_final)


def _t5_bias_tiles(rel_bias, t):
    max_exact = N_BUCKETS // 2
    d = jnp.arange(2 * t, dtype=jnp.int32)
    df = jnp.maximum(d, 1).astype(jnp.float32)
    large = max_exact + (jnp.log(df / max_exact) / math.log(MAX_DISTANCE / max_exact)
                         * (N_BUCKETS - max_exact)).astype(jnp.int32)
    bucket = jnp.where(d < max_exact, d, jnp.minimum(large, N_BUCKETS - 1))
    rb = rel_bias.astype(jnp.float32)
    table = (rb[bucket] - rb[N_BUCKETS - 1]) * LOG2E
    period = 2 * t
    k = np.arange(period)
    tiles = []
    for off in (0, t):
        idx = np.clip(np.where(k < t, off - k, off + period - k), 0, 2 * t - 1)
        seq = jnp.tile(table[idx].T, (1, t))
        tiles.append(seq[:, :t * (period - 1)].reshape(N_HEADS, t, period - 1)[:, :, :t])
    return jnp.stack(tiles, axis=1)


def _pad_cols(a, n):
    return jnp.pad(a, ((0, 0), (0, n - a.shape[1])))


def kernel(x, c, rel_bias, w_ada, b_ada, g_attn, w_in, b_forget, w_out, g_mlp, w_up, conv_w, conv_b,
           w_down, g_final):
    b, s, d = x.shape
    assert b == 1 and s % ROW_TILE == 0 and s % ATT_TILE == 0 and s % FOX_TILE == 0
    assert ATT_TILE >= MAX_DISTANCE
    depth = w_ada.shape[0]
    assert depth == 1, "the fused MLP kernel applies the final norm, so it must be the last layer"
    d_ff = w_down.shape[1]
    ff_pad = -(-d_ff // FF_TILE) * FF_TILE
    n_sel = min(TOPK_MAX, s // 4)
    gw, nq = GROUP_WIDTH, IDX_HEADS * IDX_DIM
    bf = jnp.bfloat16
    xs = x[0]
    c_col = c.reshape(d, 1)
    bias_tiles = _t5_bias_tiles(rel_bias, ATT_TILE)

    for l in range(depth):
        mod = _ada(c_col, w_ada[l], b_ada[l][None, :])
        shift_a, scale_a, gate_a, shift_m, scale_m, gate_m = [mod[:, k * d:(k + 1) * d] for k in range(6)]

        wi = w_in[l]
        o = np.cumsum([0, gw, gw, gw, nq, IDX_DIM, IDX_HEADS, gw, gw, gw, gw, N_HEADS])
        seg = lambda k: wi[:, o[k]:o[k + 1]]
        q_scale = HEAD_DIM ** -0.5 * LOG2E
        w_main = jnp.concatenate([seg(0) * q_scale, seg(1), seg(2),
                                  seg(6) * q_scale, seg(7), seg(8), seg(9)], axis=1).astype(bf)
        w_small = _pad_cols(jnp.concatenate([seg(3) * IDX_DIM ** -0.5, seg(4), seg(5), seg(10)], axis=1),
                            SMALL_WIDTH).astype(bf)

        proj, qi, small = _inproj(xs, g_attn[l][None, :], shift_a, scale_a, w_main, w_small)

        k_idx = small[:, :IDX_DIM].astype(bf)
        w_t = small[:, IDX_DIM:IDX_DIM + IDX_HEADS].T
        fb_t = small[:, IDX_DIM + IDX_HEADS:IDX_DIM + IDX_HEADS + N_HEADS].T
        f_rows = _fcum(fb_t.reshape(N_HEADS, s // LANES, LANES), b_forget[l]).reshape(N_HEADS, s)

        mask = _select(k_idx, qi.T, w_t, n_sel)
        o_a = _dsa(proj, mask, bias_tiles)
        o_b = _fox(proj, f_rows)

        wo = w_out[l].astype(bf)
        x1 = _outproj(xs, o_a, o_b, wo[:gw], wo[gw:], gate_a)

        wu = w_up[l]
        w_up_p = jnp.concatenate([_pad_cols(wu[:, :d_ff], ff_pad), _pad_cols(wu[:, d_ff:], ff_pad)], axis=1).astype(bf)
        cw = conv_w[l]
        cw_p = jnp.concatenate([_pad_cols(cw[:, :d_ff], ff_pad), _pad_cols(cw[:, d_ff:], ff_pad)], axis=1)
        cb = conv_b[l][None, :]
        cb_p = jnp.concatenate([_pad_cols(cb[:, :d_ff], ff_pad), _pad_cols(cb[:, d_ff:], ff_pad)], axis=1)
        w_down_p = jnp.pad(w_down[l], ((0, ff_pad - d_ff), (0, 0))).astype(bf)

        xs = _ffn(x1, g_mlp[l][None, :], shift_m, scale_m, w_up_p, cw_p, cb_p, w_down_p, gate_m,
                  g_final[None, :])

    return xs[None]
```

```python
import functools
import math

import jax
import jax.numpy as jnp
import numpy as np
from jax import lax
from jax.experimental import pallas as pl
from jax.experimental.pallas import tpu as pltpu

HEAD_DIM = 128
N_HEADS = 8
GROUP_WIDTH = N_HEADS * HEAD_DIM
IDX_HEADS = 8
IDX_DIM = 64
TOPK_MAX = 256
N_BUCKETS = 32
MAX_DISTANCE = 128
CONV_WIDTH = 3
EPS = 1e-6
NEG = -1e30
NEG_MASK = -(2.0 ** 100)

LANES = 128
SUBLANES = 8
VMEM_LIMIT = 56 * 1024 * 1024

ROW_TILE = 512
ATT_TILE = 512
SEL_TILE = 256
BIAS_TILE = 256
LOG2E = math.log2(math.e)
SEL_ROWS = 128
SEL_CHUNK = 512
COUNT_ROWS = 64
SHORTLIST = 12
RANK_TILE = 256
FF_TILE = 512
CONV_HALO = 16
SMALL_WIDTH = 640


def _cparams(sem):
    return pltpu.CompilerParams(dimension_semantics=sem, vmem_limit_bytes=VMEM_LIMIT)


def _ada_kernel(c_ref, w_ref, b_ref, o_ref):
    d = w_ref.shape[0]
    rows = 256

    def body(r, acc):
        sl = pl.ds(pl.multiple_of(r * rows, rows), rows)
        cc = c_ref[sl, :]
        ca = cc / (1.0 + jnp.exp(-cc))
        prod = w_ref[sl, :] * ca
        return acc + prod.reshape(rows // SUBLANES, SUBLANES, -1).sum(axis=0)

    acc = lax.fori_loop(0, d // rows, body, jnp.zeros((SUBLANES, w_ref.shape[1]), jnp.float32))
    o_ref[...] = jnp.sum(acc, axis=0, keepdims=True) + b_ref[...]


def _ada(c_col, w, b_row):
    d, n = w.shape
    tn = 1024
    return pl.pallas_call(
        _ada_kernel,
        grid=(n // tn,),
        in_specs=[pl.BlockSpec((d, 1), lambda j: (0, 0)),
                  pl.BlockSpec((d, tn), lambda j: (0, j)),
                  pl.BlockSpec((1, tn), lambda j: (0, j))],
        out_specs=pl.BlockSpec((1, tn), lambda j: (0, j)),
        out_shape=jax.ShapeDtypeStruct((1, n), jnp.float32),
        compiler_params=_cparams(("arbitrary",)),
        name="adaln",
    )(c_col, w, b_row)


def _norm_mod(x, g, shift, scale):
    y = x * lax.rsqrt(jnp.mean(x * x, axis=-1, keepdims=True) + EPS)
    return (y * g) * (1.0 + scale) + shift


def _inproj_kernel(x_ref, g_ref, sh_ref, sc_ref, wm_ref, ws_ref, om_ref, oq_ref, os_ref, h_ref):
    j = pl.program_id(1)

    @pl.when(j == 0)
    def _():
        h = _norm_mod(x_ref[...], g_ref[...], sh_ref[...], sc_ref[...]).astype(jnp.bfloat16)
        h_ref[...] = h
        small = jnp.dot(h, ws_ref[...], preferred_element_type=jnp.float32)
        nq = IDX_HEADS * IDX_DIM
        oq_ref[...] = small[:, :nq].astype(jnp.bfloat16)
        os_ref[...] = small[:, nq:]

    om_ref[...] = jnp.dot(h_ref[...], wm_ref[...],
                          preferred_element_type=jnp.float32).astype(jnp.bfloat16)


def _inproj(x, g, shift, scale, w_main, w_small):
    s, d = x.shape
    n_main = w_main.shape[1]
    tm, tn = ROW_TILE, GROUP_WIDTH
    nq = IDX_HEADS * IDX_DIM
    row = lambda i, j: (0, 0)
    return pl.pallas_call(
        _inproj_kernel,
        grid=(s // tm, n_main // tn),
        in_specs=[pl.BlockSpec((tm, d), lambda i, j: (i, 0)),
                  pl.BlockSpec((1, d), row), pl.BlockSpec((1, d), row), pl.BlockSpec((1, d), row),
                  pl.BlockSpec((d, tn), lambda i, j: (0, j)),
                  pl.BlockSpec((d, SMALL_WIDTH), row)],
        out_specs=[pl.BlockSpec((tm, tn), lambda i, j: (i, j)),
                   pl.BlockSpec((tm, nq), lambda i, j: (i, 0)),
                   pl.BlockSpec((tm, SMALL_WIDTH - nq), lambda i, j: (i, 0))],
        out_shape=[jax.ShapeDtypeStruct((s, n_main), jnp.bfloat16),
                   jax.ShapeDtypeStruct((s, nq), jnp.bfloat16),
                   jax.ShapeDtypeStruct((s, SMALL_WIDTH - nq), jnp.float32)],
        scratch_shapes=[pltpu.VMEM((tm, d), jnp.bfloat16)],
        compiler_params=_cparams(("parallel", "arbitrary")),
        name="inproj",
    )(x, g, shift, scale, w_main, w_small)


def _fcum_kernel(fb_ref, b_ref, o_ref):
    r = fb_ref.shape[1]
    hi = lax.Precision.HIGHEST
    ci = lax.broadcasted_iota(jnp.int32, (LANES, LANES), 0)
    cj = lax.broadcasted_iota(jnp.int32, (LANES, LANES), 1)
    upper = (ci <= cj).astype(jnp.float32)
    ri = lax.broadcasted_iota(jnp.int32, (r, r), 0)
    rj = lax.broadcasted_iota(jnp.int32, (r, r), 1)
    strict_lower = (rj < ri).astype(jnp.float32)
    for h in range(N_HEADS):
        z = fb_ref[h] + b_ref[h]
        lf = jnp.minimum(z, 0.0) - jnp.log(1.0 + jnp.exp(-jnp.abs(z)))
        within = jnp.dot(lf, upper, precision=hi, preferred_element_type=jnp.float32)
        before = jnp.dot(strict_lower, within, precision=hi, preferred_element_type=jnp.float32)
        o_ref[h] = within + before[:, LANES - 1:LANES]


def _fcum(fb_t, b_forget):
    h, r, _ = fb_t.shape
    return pl.pallas_call(
        _fcum_kernel,
        in_specs=[pl.BlockSpec(memory_space=pltpu.VMEM),
                  pl.BlockSpec(memory_space=pltpu.SMEM)],
        out_specs=pl.BlockSpec(memory_space=pltpu.VMEM),
        out_shape=jax.ShapeDtypeStruct((h, r, LANES), jnp.float32),
        name="forget_cumsum",
    )(fb_t, b_forget)


def _ordered_bits(v):
    return v ^ ((v >> 31) & jnp.int32(0x7FFFFFFF))


def _float_to_key(x):
    return _ordered_bits(pltpu.bitcast(x, jnp.int32))


def _key_to_float(key):
    return pltpu.bitcast(_ordered_bits(key), jnp.float32)


def _fold_rows(x, op):
    parts = [x[r:r + SUBLANES] for r in range(0, x.shape[0], SUBLANES)]
    while len(parts) > 1:
        parts = [op(parts[k], parts[k + 1]) for k in range(0, len(parts) - 1, 2)] + (
            [parts[-1]] if len(parts) % 2 else [])
    return parts[0]


def _select_kernel(k_ref, qit_ref, wt_ref, mask_ref, s_ref, top_ref, short_ref, *, seq, n_sel):
    tq = qit_ref.shape[1]
    ck = SEL_CHUNK
    i = pl.program_id(0)
    n_chunks = ((i + 1) * tq + ck - 1) // ck
    n_rest = (seq - n_chunks * ck).astype(jnp.float32)
    q_pos = i * tq + lax.broadcasted_iota(jnp.int32, (SEL_ROWS, tq), 1)
    key_iota = lax.broadcasted_iota(jnp.int32, (SEL_ROWS, tq), 0)
    wts = wt_ref[...] * (IDX_HEADS ** -0.5)
    n_slots = top_ref.shape[0]

    top_ref[...] = jnp.full(top_ref.shape, -jnp.inf, jnp.float32)

    def score_chunk(c, _):
        for r in range(0, ck, SEL_ROWS):
            r0 = pl.multiple_of(c * ck + r, SEL_ROWS)
            kc = k_ref[pl.ds(r0, SEL_ROWS), :]
            acc = jnp.zeros((SEL_ROWS, tq), jnp.float32)
            for h in range(IDX_HEADS):
                lg = jnp.dot(kc, qit_ref[h * IDX_DIM:(h + 1) * IDX_DIM, :],
                             preferred_element_type=jnp.float32)
                acc = acc + jnp.maximum(lg, 0.0) * wts[h:h + 1, :]
            sc = jnp.where(r0 + key_iota <= q_pos, acc, NEG)
            s_ref[pl.ds(r0, SEL_ROWS), :] = sc
            slot = r % n_slots
            top_ref[slot:slot + SEL_ROWS, :] = jnp.maximum(top_ref[slot:slot + SEL_ROWS, :], sc)
        return 0

    lax.fori_loop(0, n_chunks, score_chunk, 0)

    def count(pred, ref=s_ref, chunks=n_chunks):
        def body(c, accs):
            for r in range(0, ck, COUNT_ROWS):
                x = ref[pl.ds(pl.multiple_of(c * ck + r, COUNT_ROWS), COUNT_ROWS), :]
                accs = tuple(a + _fold_rows(jnp.where(f(x), 1.0, 0.0), jnp.add) for a, f in zip(accs, pred))
            return accs
        zero = jnp.zeros((SUBLANES, tq), jnp.float32)
        accs = lax.fori_loop(0, chunks, body, (zero,) * len(pred))
        return [jnp.sum(a, axis=0, keepdims=True) for a in accs]

    def count_ge(cf):
        return count([lambda x: x >= cf])[0] + jnp.where(NEG >= cf, n_rest, 0.0)

    top = top_ref[...]
    low_key = _float_to_key(jnp.min(_fold_rows(top, jnp.minimum), axis=0, keepdims=True))
    max_key = _float_to_key(jnp.max(_fold_rows(top, jnp.maximum), axis=0, keepdims=True))
    hi0 = jnp.maximum(max_key, _float_to_key(jnp.full((1, tq), NEG, jnp.float32))) + 1
    lo0 = jnp.full((1, tq), -2 ** 31, jnp.int32)
    max_passes = 96

    n_pos, n_nonneg = count([lambda x: x > 0.0, lambda x: x >= 0.0])
    at_zero = jnp.logical_and(n_pos < n_sel, n_nonneg >= n_sel)
    above_zero = n_pos >= n_sel
    below_zero = n_nonneg < n_sel
    zero_key = jnp.zeros((1, tq), jnp.int32)
    lo0 = jnp.where(below_zero, lo0, zero_key)
    c_lo0 = jnp.where(below_zero, float(seq), n_nonneg)
    hi0 = jnp.where(above_zero, hi0, jnp.where(at_zero, zero_key + 2 ** 23, zero_key - 1))
    c_hi0 = jnp.where(above_zero, 0.0, jnp.where(at_zero, n_pos, n_nonneg))
    done0 = jnp.where(at_zero, 1, 0)

    def searching(st):
        p, _, _, _, _, done = st
        return jnp.logical_and(p < max_passes, jnp.min(done) == 0)

    def search_pass(count_fn, st):
        p, lo, hi, c_lo, c_hi, done = st
        mid_val = _float_to_key(0.5 * _key_to_float(lo) + 0.5 * _key_to_float(hi))
        mid_key = (lo >> 1) + (hi >> 1) + (lo & hi & 1)
        cand = jnp.where(p == 0, low_key, jnp.where((p & 3) == 0, mid_key, mid_val))
        cand = jnp.minimum(jnp.maximum(cand, lo + 1), hi - 1)
        cnt = count_fn(_key_to_float(cand))
        active = done == 0
        up = jnp.logical_and(active, cnt >= n_sel)
        down = jnp.logical_and(active, cnt < n_sel)
        lo, c_lo = jnp.where(up, cand, lo), jnp.where(up, cnt, c_lo)
        hi, c_hi = jnp.where(down, cand, hi), jnp.where(down, cnt, c_hi)
        finished = jnp.logical_and(active, jnp.logical_or(cnt == n_sel, hi - 1 <= lo))
        return p + 1, lo, hi, c_lo, c_hi, jnp.where(finished, 1, done)

    def narrowing(st):
        p, _, _, c_lo, c_hi, done = st
        wide = jnp.logical_and(done == 0, c_lo - c_hi > SHORTLIST)
        return jnp.logical_and(p < max_passes, jnp.max(jnp.where(wide, 1, 0)) > 0)

    st = lax.while_loop(narrowing, functools.partial(search_pass, count_ge),
                        (jnp.int32(0), lo0, hi0, c_lo0, c_hi0, done0))

    p1, lo1, hi1, c_lo1, c_hi1, done1 = st
    hi_f = _key_to_float(hi1)
    short_ref[...] = jnp.full(short_ref.shape, -jnp.inf, jnp.float32)

    def shortlist_chunk(c, _):
        for r in range(0, ck, COUNT_ROWS):
            x = s_ref[pl.ds(pl.multiple_of(c * ck + r, COUNT_ROWS), COUNT_ROWS), :]
            y = jnp.where(x < hi_f, x, -jnp.inf)
            first = short_ref[r:r + COUNT_ROWS, :]
            second = short_ref[ck + r:ck + r + COUNT_ROWS, :]
            short_ref[r:r + COUNT_ROWS, :] = jnp.maximum(first, y)
            short_ref[ck + r:ck + r + COUNT_ROWS, :] = jnp.maximum(second, jnp.minimum(first, y))
        return 0

    lax.fori_loop(0, n_chunks, shortlist_chunk, 0)
    lo_f = _key_to_float(lo1)
    listed = count([lambda x: x >= lo_f], short_ref, short_ref.shape[0] // ck)[0]
    complete = listed == c_lo1 - c_hi1

    def count_short(cf):
        return c_hi1 + count([lambda x: x >= cf], short_ref, short_ref.shape[0] // ck)[0]

    st = lax.while_loop(searching, functools.partial(search_pass, count_short),
                        (p1, lo1, hi1, c_lo1, c_hi1, jnp.where(complete, done1, 1)))
    p2, lo2, hi2, c_lo2, c_hi2, done2 = st
    pick = lambda a, b: jnp.where(complete, a, b)

    st = lax.while_loop(searching, functools.partial(search_pass, count_ge),
                        (p2, pick(lo2, lo1), pick(hi2, hi1), pick(c_lo2, c_lo1), pick(c_hi2, c_hi1),
                         pick(done2, done1)))
    _, lo, _, c_lo, c_hi, _ = st
    thr = _key_to_float(lo)

    need = jnp.where(c_lo > n_sel, n_sel - c_hi, float(seq))
    rt = RANK_TILE
    earlier = (lax.broadcasted_iota(jnp.int32, (rt, rt), 1)
               < lax.broadcasted_iota(jnp.int32, (rt, rt), 0)).astype(jnp.bfloat16)

    def mask_chunk(c, ties_before):
        for t0 in range(0, ck, rt):
            c0 = pl.multiple_of(c * ck + t0, rt)
            tie = jnp.where(s_ref[pl.ds(c0, rt), :] == thr, 1.0, 0.0)
            rank = ties_before + jnp.dot(earlier, tie.astype(jnp.bfloat16),
                                         preferred_element_type=jnp.float32)
            for r in range(0, rt, SEL_ROWS):
                r0 = pl.multiple_of(c0 + r, SEL_ROWS)
                x = s_ref[pl.ds(r0, SEL_ROWS), :]
                kept_tie = jnp.logical_and(x == thr, rank[r:r + SEL_ROWS] < need)
                keep = jnp.logical_and(jnp.logical_or(x > thr, kept_tie), r0 + key_iota <= q_pos)
                mask_ref[:, pl.ds(r0, SEL_ROWS)] = jnp.where(keep, 0.0, NEG_MASK).T.astype(jnp.bfloat16)
            ties_before = ties_before + jnp.sum(_fold_rows(tie, jnp.add), axis=0, keepdims=True)
        return ties_before

    lax.fori_loop(0, n_chunks, mask_chunk, jnp.zeros((1, tq), jnp.float32))

    def fill_chunk(c, _):
        r0 = pl.multiple_of(c * ck, ck)
        mask_ref[:, pl.ds(r0, ck)] = jnp.full((tq, ck), NEG_MASK, jnp.bfloat16)
        return 0

    lax.fori_loop(n_chunks, seq // ck, fill_chunk, 0)


def _select(k_idx, qi_t, w_t, n_sel):
    s = k_idx.shape[0]
    tq = SEL_TILE
    assert n_sel <= RANK_TILE and s % SEL_CHUNK == 0 and SEL_CHUNK % RANK_TILE == 0
    return pl.pallas_call(
        functools.partial(_select_kernel, seq=s, n_sel=n_sel),
        grid=(s // tq,),
        in_specs=[pl.BlockSpec(k_idx.shape, lambda i: (0, 0)),
                  pl.BlockSpec((qi_t.shape[0], tq), lambda i: (0, i)),
                  pl.BlockSpec((IDX_HEADS, tq), lambda i: (0, i))],
        out_specs=pl.BlockSpec((tq, s), lambda i: (i, 0)),
        out_shape=jax.ShapeDtypeStruct((s, s), jnp.bfloat16),
        scratch_shapes=[pltpu.VMEM((s, tq), jnp.float32), pltpu.VMEM((RANK_TILE, tq), jnp.float32),
                        pltpu.VMEM((2 * SEL_CHUNK, tq), jnp.float32)],
        compiler_params=_cparams(("parallel",)),
        name="dsa_select",
    )(k_idx, qi_t, w_t)


def _qk(a, b):
    return lax.dot_general(a, b, (((1,), (1,)), ((), ())), preferred_element_type=jnp.float32)


def _pair_tables(nq):
    qt = np.concatenate([np.full(q + 1, q, np.int32) for q in range(nq)])
    st = np.concatenate([np.arange(q + 1, dtype=np.int32) for q in range(nq)])
    return jnp.asarray(qt), jnp.asarray(st)


def _softmax_step(h, s, v, m_ref, acc_ref):
    ts = s.shape[1]
    m_prev = m_ref[h]
    m_new = jnp.maximum(m_prev, jnp.max(s, axis=-1, keepdims=True))
    alpha = jnp.exp2(m_prev - m_new)
    pm = jnp.exp2(s - jnp.tile(m_new, (1, ts // LANES))).astype(jnp.bfloat16)
    v1 = jnp.concatenate([v, jnp.ones((ts, HEAD_DIM), jnp.bfloat16)], axis=1)
    acc_ref[h] = (jnp.tile(alpha, (1, 2 * HEAD_DIM // LANES)) * acc_ref[h]
                  + jnp.dot(pm, v1, preferred_element_type=jnp.float32))
    m_ref[h] = m_new


def _init_softmax(m_ref, acc_ref):
    m_ref[...] = jnp.full(m_ref.shape, NEG_MASK, jnp.float32)
    acc_ref[...] = jnp.zeros(acc_ref.shape, jnp.float32)


def _softmax_result(h, acc_ref):
    return acc_ref[h, :, :HEAD_DIM] / acc_ref[h, :, HEAD_DIM:]


def _att_scratch(tq):
    return [pltpu.VMEM((N_HEADS, tq, LANES), jnp.float32),
            pltpu.VMEM((N_HEADS, tq, 2 * HEAD_DIM), jnp.float32)]


def _dsa_kernel(qt_ref, st_ref, q_ref, k_ref, v_ref, mask_ref, bias_ref, o_ref, m_ref, acc_ref):
    p = pl.program_id(0)
    qi, si = qt_ref[p], st_ref[p]
    near = qi - si < 2
    diag = qi == si
    zero, lag0, lag1 = 2, 0, 1
    quad = [[jnp.where(diag, lag0, zero), jnp.where(diag, zero, lag1)],
            [jnp.where(diag, lag1, zero), jnp.where(diag, lag0, zero)]]

    @pl.when(si == 0)
    def _():
        _init_softmax(m_ref, acc_ref)

    def sweep(with_bias):
        maskf = mask_ref[...].astype(jnp.float32)
        for h in range(N_HEADS):
            hs = slice(h * HEAD_DIM, (h + 1) * HEAD_DIM)
            s = _qk(q_ref[:, hs], k_ref[:, hs]) + maskf
            if with_bias:
                s = s + jnp.concatenate(
                    [jnp.concatenate([bias_ref[h, quad[a][b]] for b in range(2)], axis=1)
                     for a in range(2)], axis=0)
            _softmax_step(h, s, v_ref[:, hs], m_ref, acc_ref)

    pl.when(near)(functools.partial(sweep, True))
    pl.when(jnp.logical_not(near))(functools.partial(sweep, False))

    @pl.when(si == qi)
    def _():
        for h in range(N_HEADS):
            hs = slice(h * HEAD_DIM, (h + 1) * HEAD_DIM)
            o_ref[:, hs] = _softmax_result(h, acc_ref).astype(o_ref.dtype)


def _dsa(proj, mask, bias_tiles):
    s = proj.shape[0]
    t = ATT_TILE
    assert t == 2 * BIAS_TILE and bias_tiles.shape[1:] == (3, BIAS_TILE, BIAS_TILE)
    qt, st = _pair_tables(s // t)
    w = GROUP_WIDTH
    grid_spec = pltpu.PrefetchScalarGridSpec(
        num_scalar_prefetch=2,
        grid=(qt.shape[0],),
        in_specs=[pl.BlockSpec((t, w), lambda p, qt, st: (qt[p], 0)),
                  pl.BlockSpec((t, w), lambda p, qt, st: (st[p], 1)),
                  pl.BlockSpec((t, w), lambda p, qt, st: (st[p], 2)),
                  pl.BlockSpec((t, t), lambda p, qt, st: (qt[p], st[p])),
                  pl.BlockSpec(bias_tiles.shape, lambda p, qt, st: (0, 0, 0, 0))],
        out_specs=pl.BlockSpec((t, w), lambda p, qt, st: (qt[p], 0)),
        scratch_shapes=_att_scratch(t))
    return pl.pallas_call(
        _dsa_kernel, grid_spec=grid_spec,
        out_shape=jax.ShapeDtypeStruct((s, w), jnp.bfloat16),
        compiler_params=_cparams(("arbitrary",)),
        name="dsa_attention",
    )(qt, st, proj, proj, proj, mask, bias_tiles)


def _fox_kernel(qt_ref, st_ref, q_ref, k_ref, v_ref, g_ref, fq_ref, fs_ref, o_ref, m_ref, acc_ref):
    p = pl.program_id(0)
    qi, si = qt_ref[p], st_ref[p]
    tq, ts = q_ref.shape[0], k_ref.shape[0]

    @pl.when(si == 0)
    def _():
        _init_softmax(m_ref, acc_ref)

    def sweep(diagonal):
        if diagonal:
            causal = (lax.broadcasted_iota(jnp.int32, (tq, ts), 1)
                      <= lax.broadcasted_iota(jnp.int32, (tq, ts), 0))
        for h in range(N_HEADS):
            hs = slice(h * HEAD_DIM, (h + 1) * HEAD_DIM)
            decay = (fq_ref[h:h + 1, 0:1] - fs_ref[h:h + 1, :]) * LOG2E
            s = _qk(q_ref[:, hs], k_ref[:, hs]) + decay
            if diagonal:
                s = jnp.where(causal, s, NEG_MASK)
            _softmax_step(h, s, v_ref[:, hs], m_ref, acc_ref)

    pl.when(si == qi)(functools.partial(sweep, True))
    pl.when(si != qi)(functools.partial(sweep, False))

    @pl.when(si == qi)
    def _():
        for h in range(N_HEADS):
            hs = slice(h * HEAD_DIM, (h + 1) * HEAD_DIM)
            g = g_ref[:, hs].astype(jnp.float32)
            o_ref[:, hs] = (_softmax_result(h, acc_ref) / (1.0 + jnp.exp(-g))).astype(o_ref.dtype)


def _fox(proj, f_rows):
    s = proj.shape[0]
    t = ATT_TILE
    qt, st = _pair_tables(s // t)
    w = GROUP_WIDTH
    grid_spec = pltpu.PrefetchScalarGridSpec(
        num_scalar_prefetch=2,
        grid=(qt.shape[0],),
        in_specs=[pl.BlockSpec((t, w), lambda p, qt, st: (qt[p], 3)),
                  pl.BlockSpec((t, w), lambda p, qt, st: (st[p], 4)),
                  pl.BlockSpec((t, w), lambda p, qt, st: (st[p], 5)),
                  pl.BlockSpec((t, w), lambda p, qt, st: (qt[p], 6)),
                  pl.BlockSpec((N_HEADS, t), lambda p, qt, st: (0, qt[p])),
                  pl.BlockSpec((N_HEADS, t), lambda p, qt, st: (0, st[p]))],
        out_specs=pl.BlockSpec((t, w), lambda p, qt, st: (qt[p], 0)),
        scratch_shapes=_att_scratch(t))
    return pl.pallas_call(
        _fox_kernel, grid_spec=grid_spec,
        out_shape=jax.ShapeDtypeStruct((s, w), jnp.bfloat16),
        compiler_params=_cparams(("arbitrary",)),
        name="fox_attention",
    )(qt, st, proj, proj, proj, proj, f_rows, f_rows)


def _outproj_kernel(x_ref, oa_ref, ob_ref, wa_ref, wb_ref, gate_ref, o_ref):
    y = jnp.dot(oa_ref[...], wa_ref[...], preferred_element_type=jnp.float32)
    y = y + jnp.dot(ob_ref[...], wb_ref[...], preferred_element_type=jnp.float32)
    o_ref[...] = x_ref[...] + gate_ref[...] * y


def _outproj(x, o_a, o_b, w_a, w_b, gate):
    s, d = x.shape
    tm, tn = ROW_TILE, 1024
    w = GROUP_WIDTH
    return pl.pallas_call(
        _outproj_kernel,
        grid=(s // tm, d // tn),
        in_specs=[pl.BlockSpec((tm, tn), lambda i, j: (i, j)),
                  pl.BlockSpec((tm, w), lambda i, j: (i, 0)),
                  pl.BlockSpec((tm, w), lambda i, j: (i, 0)),
                  pl.BlockSpec((w, tn), lambda i, j: (0, j)),
                  pl.BlockSpec((w, tn), lambda i, j: (0, j)),
                  pl.BlockSpec((1, tn), lambda i, j: (0, j))],
        out_specs=pl.BlockSpec((tm, tn), lambda i, j: (i, j)),
        out_shape=jax.ShapeDtypeStruct((s, d), jnp.float32),
        compiler_params=_cparams(("parallel", "arbitrary")),
        name="outproj",
    )(x, o_a, o_b, w_a, w_b, gate)


def _conv(y_ref, cw_ref, cb_ref, tm):
    h = CONV_HALO
    return (cw_ref[2:3, :] * y_ref[h:h + tm, :] + cw_ref[1:2, :] * y_ref[h - 1:h - 1 + tm, :]
            + cw_ref[0:1, :] * y_ref[h - 2:h - 2 + tm, :] + cb_ref[...])


def _ffn_kernel(x_ref, xh_ref, g_ref, sh_ref, sc_ref, wg_ref, wv_ref, cwg_ref, cwv_ref, cbg_ref, cbv_ref,
                wd_ref, gate_ref, gf_ref, o_ref, h_ref, yg_ref, yv_ref, acc_ref):
    i, f = pl.program_id(0), pl.program_id(1)
    tm = x_ref.shape[0]

    @pl.when(f == 0)
    def _():
        halo = _norm_mod(xh_ref[...], g_ref[...], sh_ref[...], sc_ref[...])
        h_ref[0:CONV_HALO, :] = jnp.where(i > 0, halo, 0.0).astype(jnp.bfloat16)
        h_ref[CONV_HALO:, :] = _norm_mod(x_ref[...], g_ref[...], sh_ref[...], sc_ref[...]).astype(jnp.bfloat16)
        acc_ref[...] = jnp.zeros(acc_ref.shape, jnp.float32)

    hb = h_ref[...]
    yg_ref[...] = jnp.dot(hb, wg_ref[...], preferred_element_type=jnp.float32)
    yv_ref[...] = jnp.dot(hb, wv_ref[...], preferred_element_type=jnp.float32)
    ug = _conv(yg_ref, cwg_ref, cbg_ref, tm)
    uv = _conv(yv_ref, cwv_ref, cbv_ref, tm)
    a = (ug / (1.0 + jnp.exp(-ug))) * uv
    acc_ref[...] += jnp.dot(a.astype(jnp.bfloat16), wd_ref[...], preferred_element_type=jnp.float32)

    @pl.when(f == pl.num_programs(1) - 1)
    def _():
        x2 = x_ref[...] + gate_ref[...] * acc_ref[...]
        y = x2 * lax.rsqrt(jnp.mean(x2 * x2, axis=-1, keepdims=True) + EPS)
        o_ref[...] = y * gf_ref[...]


def _ffn(x, g, shift, scale, w_up, conv_w, conv_b, w_down, gate, g_final):
    s, d = x.shape
    fp = w_down.shape[0]
    tm, tf = ROW_TILE, FF_TILE
    nf = fp // tf
    hb = tm // CONV_HALO
    row = lambda i, f: (0, 0)
    return pl.pallas_call(
        _ffn_kernel,
        grid=(s // tm, nf),
        in_specs=[pl.BlockSpec((tm, d), lambda i, f: (i, 0)),
                  pl.BlockSpec((CONV_HALO, d), lambda i, f: (jnp.maximum(i * hb - 1, 0), 0)),
                  pl.BlockSpec((1, d), row), pl.BlockSpec((1, d), row), pl.BlockSpec((1, d), row),
                  pl.BlockSpec((d, tf), lambda i, f: (0, f)),
                  pl.BlockSpec((d, tf), lambda i, f: (0, f + nf)),
                  pl.BlockSpec((CONV_WIDTH, tf), lambda i, f: (0, f)),
                  pl.BlockSpec((CONV_WIDTH, tf), lambda i, f: (0, f + nf)),
                  pl.BlockSpec((1, tf), lambda i, f: (0, f)),
                  pl.BlockSpec((1, tf), lambda i, f: (0, f + nf)),
                  pl.BlockSpec((tf, d), lambda i, f: (f, 0)),
                  pl.BlockSpec((1, d), row), pl.BlockSpec((1, d), row)],
        out_specs=pl.BlockSpec((tm, d), lambda i, f: (i, 0)),
        out_shape=jax.ShapeDtypeStruct((s, d), jnp.float32),
        scratch_shapes=[pltpu.VMEM((tm + CONV_HALO, d), jnp.bfloat16),
                        pltpu.VMEM((tm + CONV_HALO, tf), jnp.float32),
                        pltpu.VMEM((tm + CONV_HALO, tf), jnp.float32),
                        pltpu.VMEM((tm, d), jnp.float32)],
        compiler_params=_cparams(("parallel", "arbitrary")),
        name="conv_ffn",
    )(x, x, g, shift, scale, w_up, w_up, conv_w, conv_w, conv_b, conv_b, w_down, gate, g_final)


def _t5_bias_tiles(rel_bias, t):
    max_exact = N_BUCKETS // 2
    d = jnp.arange(2 * t, dtype=jnp.int32)
    df = jnp.maximum(d, 1).astype(jnp.float32)
    large = max_exact + (jnp.log(df / max_exact) / math.log(MAX_DISTANCE / max_exact)
                         * (N_BUCKETS - max_exact)).astype(jnp.int32)
    bucket = jnp.where(d < max_exact, d, jnp.minimum(large, N_BUCKETS - 1))
    rb = rel_bias.astype(jnp.float32)
    table = (rb[bucket] - rb[N_BUCKETS - 1]) * LOG2E
    period = 2 * t
    k = np.arange(period)
    tiles = []
    for off in (0, t):
        idx = np.clip(np.where(k < t, off - k, off + period - k), 0, 2 * t - 1)
        seq = jnp.tile(table[idx].T, (1, t))
        tiles.append(seq[:, :t * (period - 1)].reshape(N_HEADS, t, period - 1)[:, :, :t])
    return jnp.stack(tiles + [jnp.zeros_like(tiles[0])], axis=1)


def _pad_cols(a, n):
    return jnp.pad(a, ((0, 0), (0, n - a.shape[1])))


def kernel(x, c, rel_bias, w_ada, b_ada, g_attn, w_in, b_forget, w_out, g_mlp, w_up, conv_w, conv_b,
           w_down, g_final):
    b, s, d = x.shape
    assert b == 1 and s % ROW_TILE == 0 and s % ATT_TILE == 0 and s % SEL_TILE == 0
    assert BIAS_TILE >= MAX_DISTANCE
    depth = w_ada.shape[0]
    assert depth == 1, "the fused MLP kernel applies the final norm, so it must be the last layer"
    d_ff = w_down.shape[1]
    ff_pad = -(-d_ff // FF_TILE) * FF_TILE
    n_sel = min(TOPK_MAX, s // 4)
    gw, nq = GROUP_WIDTH, IDX_HEADS * IDX_DIM
    bf = jnp.bfloat16
    xs = x[0]
    c_col = c.reshape(d, 1)
    bias_tiles = _t5_bias_tiles(rel_bias, BIAS_TILE)

    for l in range(depth):
        mod = _ada(c_col, w_ada[l], b_ada[l][None, :])
        shift_a, scale_a, gate_a, shift_m, scale_m, gate_m = [mod[:, k * d:(k + 1) * d] for k in range(6)]

        wi = w_in[l]
        o = np.cumsum([0, gw, gw, gw, nq, IDX_DIM, IDX_HEADS, gw, gw, gw, gw, N_HEADS])
        seg = lambda k: wi[:, o[k]:o[k + 1]]
        q_scale = HEAD_DIM ** -0.5 * LOG2E
        w_main = jnp.concatenate([seg(0) * q_scale, seg(1), seg(2),
                                  seg(6) * q_scale, seg(7), seg(8), seg(9)], axis=1).astype(bf)
        w_small = _pad_cols(jnp.concatenate([seg(3) * IDX_DIM ** -0.5, seg(4), seg(5), seg(10)], axis=1),
                            SMALL_WIDTH).astype(bf)

        proj, qi, small = _inproj(xs, g_attn[l][None, :], shift_a, scale_a, w_main, w_small)

        k_idx = small[:, :IDX_DIM].astype(bf)
        w_t = small[:, IDX_DIM:IDX_DIM + IDX_HEADS].T
        fb_t = small[:, IDX_DIM + IDX_HEADS:IDX_DIM + IDX_HEADS + N_HEADS].T
        f_rows = _fcum(fb_t.reshape(N_HEADS, s // LANES, LANES), b_forget[l]).reshape(N_HEADS, s)

        mask = _select(k_idx, qi.T, w_t, n_sel)
        o_a = _dsa(proj, mask, bias_tiles)
        o_b = _fox(proj, f_rows)

        wo = w_out[l].astype(bf)
        x1 = _outproj(xs, o_a, o_b, wo[:gw], wo[gw:], gate_a)

        wu = w_up[l]
        w_up_p = jnp.concatenate([_pad_cols(wu[:, :d_ff], ff_pad), _pad_cols(wu[:, d_ff:], ff_pad)], axis=1).astype(bf)
        cw = conv_w[l]
        cw_p = jnp.concatenate([_pad_cols(cw[:, :d_ff], ff_pad), _pad_cols(cw[:, d_ff:], ff_pad)], axis=1)
        cb = conv_b[l][None, :]
        cb_p = jnp.concatenate([_pad_cols(cb[:, :d_ff], ff_pad), _pad_cols(cb[:, d_ff:], ff_pad)], axis=1)
        w_down_p = jnp.pad(w_down[l], ((0, ff_pad - d_ff), (0, 0))).astype(bf)

        xs = _ffn(x1, g_mlp[l][None, :], shift_m, scale_m, w_up_p, cw_p, cb_p, w_down_p, gate_m,
                  g_final[None, :])

    return xs[None]
```

```python
import functools
import math

import jax
import jax.numpy as jnp
import numpy as np
from jax import lax
from jax.experimental import pallas as pl
from jax.experimental.pallas import tpu as pltpu

HEAD_DIM = 128
N_HEADS = 8
GROUP_WIDTH = N_HEADS * HEAD_DIM
IDX_HEADS = 8
IDX_DIM = 64
TOPK_MAX = 256
N_BUCKETS = 32
MAX_DISTANCE = 128
CONV_WIDTH = 3
EPS = 1e-6
NEG = -1e30
NEG_MASK = -(2.0 ** 100)

LANES = 128
SUBLANES = 8
VMEM_LIMIT = 56 * 1024 * 1024

ROW_TILE = 512
ATT_TILE = 512
SEL_TILE = 256
BIAS_TILE = 256
LOG2E = math.log2(math.e)
SEL_ROWS = 128
SEL_CHUNK = 512
COUNT_ROWS = 64
SHORTLIST = 12
SKIP_LOG2 = 160.0
NORM_SLACK = 1.01
RANK_TILE = 256
FF_TILE = 512
CONV_HALO = 16
SMALL_WIDTH = 640


def _cparams(sem):
    return pltpu.CompilerParams(dimension_semantics=sem, vmem_limit_bytes=VMEM_LIMIT)


def _ada_kernel(c_ref, w_ref, b_ref, o_ref):
    d = w_ref.shape[0]
    rows = 256

    def body(r, acc):
        sl = pl.ds(pl.multiple_of(r * rows, rows), rows)
        cc = c_ref[sl, :]
        ca = cc / (1.0 + jnp.exp(-cc))
        prod = w_ref[sl, :] * ca
        return acc + prod.reshape(rows // SUBLANES, SUBLANES, -1).sum(axis=0)

    acc = lax.fori_loop(0, d // rows, body, jnp.zeros((SUBLANES, w_ref.shape[1]), jnp.float32))
    o_ref[...] = jnp.sum(acc, axis=0, keepdims=True) + b_ref[...]


def _ada(c_col, w, b_row):
    d, n = w.shape
    tn = 1024
    return pl.pallas_call(
        _ada_kernel,
        grid=(n // tn,),
        in_specs=[pl.BlockSpec((d, 1), lambda j: (0, 0)),
                  pl.BlockSpec((d, tn), lambda j: (0, j)),
                  pl.BlockSpec((1, tn), lambda j: (0, j))],
        out_specs=pl.BlockSpec((1, tn), lambda j: (0, j)),
        out_shape=jax.ShapeDtypeStruct((1, n), jnp.float32),
        compiler_params=_cparams(("arbitrary",)),
        name="adaln",
    )(c_col, w, b_row)


def _norm_mod(x, g, shift, scale):
    y = x * lax.rsqrt(jnp.mean(x * x, axis=-1, keepdims=True) + EPS)
    return (y * g) * (1.0 + scale) + shift


def _inproj_kernel(x_ref, g_ref, sh_ref, sc_ref, wm_ref, ws_ref, om_ref, oq_ref, os_ref, h_ref):
    j = pl.program_id(1)

    @pl.when(j == 0)
    def _():
        h = _norm_mod(x_ref[...], g_ref[...], sh_ref[...], sc_ref[...]).astype(jnp.bfloat16)
        h_ref[...] = h
        small = jnp.dot(h, ws_ref[...], preferred_element_type=jnp.float32)
        nq = IDX_HEADS * IDX_DIM
        oq_ref[...] = small[:, :nq].astype(jnp.bfloat16)
        os_ref[...] = small[:, nq:]

    om_ref[...] = jnp.dot(h_ref[...], wm_ref[...],
                          preferred_element_type=jnp.float32).astype(jnp.bfloat16)


def _inproj(x, g, shift, scale, w_main, w_small):
    s, d = x.shape
    n_main = w_main.shape[1]
    tm, tn = ROW_TILE, GROUP_WIDTH
    nq = IDX_HEADS * IDX_DIM
    row = lambda i, j: (0, 0)
    return pl.pallas_call(
        _inproj_kernel,
        grid=(s // tm, n_main // tn),
        in_specs=[pl.BlockSpec((tm, d), lambda i, j: (i, 0)),
                  pl.BlockSpec((1, d), row), pl.BlockSpec((1, d), row), pl.BlockSpec((1, d), row),
                  pl.BlockSpec((d, tn), lambda i, j: (0, j)),
                  pl.BlockSpec((d, SMALL_WIDTH), row)],
        out_specs=[pl.BlockSpec((tm, tn), lambda i, j: (i, j)),
                   pl.BlockSpec((tm, nq), lambda i, j: (i, 0)),
                   pl.BlockSpec((tm, SMALL_WIDTH - nq), lambda i, j: (i, 0))],
        out_shape=[jax.ShapeDtypeStruct((s, n_main), jnp.bfloat16),
                   jax.ShapeDtypeStruct((s, nq), jnp.bfloat16),
                   jax.ShapeDtypeStruct((s, SMALL_WIDTH - nq), jnp.float32)],
        scratch_shapes=[pltpu.VMEM((tm, d), jnp.bfloat16)],
        compiler_params=_cparams(("parallel", "arbitrary")),
        name="inproj",
    )(x, g, shift, scale, w_main, w_small)


def _fcum_kernel(fb_ref, b_ref, o_ref):
    r = fb_ref.shape[1]
    hi = lax.Precision.HIGHEST
    ci = lax.broadcasted_iota(jnp.int32, (LANES, LANES), 0)
    cj = lax.broadcasted_iota(jnp.int32, (LANES, LANES), 1)
    upper = (ci <= cj).astype(jnp.float32)
    ri = lax.broadcasted_iota(jnp.int32, (r, r), 0)
    rj = lax.broadcasted_iota(jnp.int32, (r, r), 1)
    strict_lower = (rj < ri).astype(jnp.float32)
    for h in range(N_HEADS):
        z = fb_ref[h] + b_ref[h]
        lf = jnp.minimum(z, 0.0) - jnp.log(1.0 + jnp.exp(-jnp.abs(z)))
        within = jnp.dot(lf, upper, precision=hi, preferred_element_type=jnp.float32)
        before = jnp.dot(strict_lower, within, precision=hi, preferred_element_type=jnp.float32)
        o_ref[h] = within + before[:, LANES - 1:LANES]


def _fcum(fb_t, b_forget):
    h, r, _ = fb_t.shape
    return pl.pallas_call(
        _fcum_kernel,
        in_specs=[pl.BlockSpec(memory_space=pltpu.VMEM),
                  pl.BlockSpec(memory_space=pltpu.SMEM)],
        out_specs=pl.BlockSpec(memory_space=pltpu.VMEM),
        out_shape=jax.ShapeDtypeStruct((h, r, LANES), jnp.float32),
        name="forget_cumsum",
    )(fb_t, b_forget)


def _ordered_bits(v):
    return v ^ ((v >> 31) & jnp.int32(0x7FFFFFFF))


def _float_to_key(x):
    return _ordered_bits(pltpu.bitcast(x, jnp.int32))


def _key_to_float(key):
    return pltpu.bitcast(_ordered_bits(key), jnp.float32)


def _fold_rows(x, op):
    parts = [x[r:r + SUBLANES] for r in range(0, x.shape[0], SUBLANES)]
    while len(parts) > 1:
        parts = [op(parts[k], parts[k + 1]) for k in range(0, len(parts) - 1, 2)] + (
            [parts[-1]] if len(parts) % 2 else [])
    return parts[0]


def _select_kernel(k_ref, qit_ref, wt_ref, mask_ref, s_ref, top_ref, short_ref, *, seq, n_sel):
    tq = qit_ref.shape[1]
    ck = SEL_CHUNK
    i = pl.program_id(0)
    n_chunks = ((i + 1) * tq + ck - 1) // ck
    n_rest = (seq - n_chunks * ck).astype(jnp.float32)
    q_pos = i * tq + lax.broadcasted_iota(jnp.int32, (SEL_ROWS, tq), 1)
    key_iota = lax.broadcasted_iota(jnp.int32, (SEL_ROWS, tq), 0)
    wts = wt_ref[...] * (IDX_HEADS ** -0.5)
    n_slots = top_ref.shape[0]

    top_ref[...] = jnp.full(top_ref.shape, -jnp.inf, jnp.float32)

    def score_chunk(c, _):
        for r in range(0, ck, SEL_ROWS):
            r0 = pl.multiple_of(c * ck + r, SEL_ROWS)
            kc = k_ref[pl.ds(r0, SEL_ROWS), :]
            acc = jnp.zeros((SEL_ROWS, tq), jnp.float32)
            for h in range(IDX_HEADS):
                lg = jnp.dot(kc, qit_ref[h * IDX_DIM:(h + 1) * IDX_DIM, :],
                             preferred_element_type=jnp.float32)
                acc = acc + jnp.maximum(lg, 0.0) * wts[h:h + 1, :]
            sc = jnp.where(r0 + key_iota <= q_pos, acc, NEG)
            s_ref[pl.ds(r0, SEL_ROWS), :] = sc
            slot = r % n_slots
            top_ref[slot:slot + SEL_ROWS, :] = jnp.maximum(top_ref[slot:slot + SEL_ROWS, :], sc)
        return 0

    lax.fori_loop(0, n_chunks, score_chunk, 0)

    def count(pred, ref=s_ref, chunks=n_chunks):
        def body(c, accs):
            for r in range(0, ck, COUNT_ROWS):
                x = ref[pl.ds(pl.multiple_of(c * ck + r, COUNT_ROWS), COUNT_ROWS), :]
                accs = tuple(a + _fold_rows(jnp.where(f(x), 1.0, 0.0), jnp.add) for a, f in zip(accs, pred))
            return accs
        zero = jnp.zeros((SUBLANES, tq), jnp.float32)
        accs = lax.fori_loop(0, chunks, body, (zero,) * len(pred))
        return [jnp.sum(a, axis=0, keepdims=True) for a in accs]

    def count_ge(cf):
        return count([lambda x: x >= cf])[0] + jnp.where(NEG >= cf, n_rest, 0.0)

    top = top_ref[...]
    low_key = _float_to_key(jnp.min(_fold_rows(top, jnp.minimum), axis=0, keepdims=True))
    max_key = _float_to_key(jnp.max(_fold_rows(top, jnp.maximum), axis=0, keepdims=True))
    hi0 = jnp.maximum(max_key, _float_to_key(jnp.full((1, tq), NEG, jnp.float32))) + 1
    lo0 = jnp.full((1, tq), -2 ** 31, jnp.int32)
    max_passes = 96

    n_pos, n_nonneg = count([lambda x: x > 0.0, lambda x: x >= 0.0])
    at_zero = jnp.logical_and(n_pos < n_sel, n_nonneg >= n_sel)
    above_zero = n_pos >= n_sel
    below_zero = n_nonneg < n_sel
    zero_key = jnp.zeros((1, tq), jnp.int32)
    lo0 = jnp.where(below_zero, lo0, zero_key)
    c_lo0 = jnp.where(below_zero, float(seq), n_nonneg)
    hi0 = jnp.where(above_zero, hi0, jnp.where(at_zero, zero_key + 2 ** 23, zero_key - 1))
    c_hi0 = jnp.where(above_zero, 0.0, jnp.where(at_zero, n_pos, n_nonneg))
    done0 = jnp.where(at_zero, 1, 0)

    def searching(st):
        p, _, _, _, _, done = st
        return jnp.logical_and(p < max_passes, jnp.min(done) == 0)

    def search_pass(count_fn, st):
        p, lo, hi, c_lo, c_hi, done = st
        mid_val = _float_to_key(0.5 * _key_to_float(lo) + 0.5 * _key_to_float(hi))
        mid_key = (lo >> 1) + (hi >> 1) + (lo & hi & 1)
        cand = jnp.where(p == 0, low_key, jnp.where((p & 3) == 0, mid_key, mid_val))
        cand = jnp.minimum(jnp.maximum(cand, lo + 1), hi - 1)
        cnt = count_fn(_key_to_float(cand))
        active = done == 0
        up = jnp.logical_and(active, cnt >= n_sel)
        down = jnp.logical_and(active, cnt < n_sel)
        lo, c_lo = jnp.where(up, cand, lo), jnp.where(up, cnt, c_lo)
        hi, c_hi = jnp.where(down, cand, hi), jnp.where(down, cnt, c_hi)
        finished = jnp.logical_and(active, jnp.logical_or(cnt == n_sel, hi - 1 <= lo))
        return p + 1, lo, hi, c_lo, c_hi, jnp.where(finished, 1, done)

    def narrowing(st):
        p, _, _, c_lo, c_hi, done = st
        wide = jnp.logical_and(done == 0, c_lo - c_hi > SHORTLIST)
        return jnp.logical_and(p < max_passes, jnp.max(jnp.where(wide, 1, 0)) > 0)

    st = lax.while_loop(narrowing, functools.partial(search_pass, count_ge),
                        (jnp.int32(0), lo0, hi0, c_lo0, c_hi0, done0))

    p1, lo1, hi1, c_lo1, c_hi1, done1 = st
    hi_f = _key_to_float(hi1)
    short_ref[...] = jnp.full(short_ref.shape, -jnp.inf, jnp.float32)

    def shortlist_chunk(c, _):
        for r in range(0, ck, COUNT_ROWS):
            x = s_ref[pl.ds(pl.multiple_of(c * ck + r, COUNT_ROWS), COUNT_ROWS), :]
            y = jnp.where(x < hi_f, x, -jnp.inf)
            first = short_ref[r:r + COUNT_ROWS, :]
            second = short_ref[ck + r:ck + r + COUNT_ROWS, :]
            short_ref[r:r + COUNT_ROWS, :] = jnp.maximum(first, y)
            short_ref[ck + r:ck + r + COUNT_ROWS, :] = jnp.maximum(second, jnp.minimum(first, y))
        return 0

    lax.fori_loop(0, n_chunks, shortlist_chunk, 0)
    lo_f = _key_to_float(lo1)
    listed = count([lambda x: x >= lo_f], short_ref, short_ref.shape[0] // ck)[0]
    complete = listed == c_lo1 - c_hi1

    def count_short(cf):
        return c_hi1 + count([lambda x: x >= cf], short_ref, short_ref.shape[0] // ck)[0]

    st = lax.while_loop(searching, functools.partial(search_pass, count_short),
                        (p1, lo1, hi1, c_lo1, c_hi1, jnp.where(complete, done1, 1)))
    p2, lo2, hi2, c_lo2, c_hi2, done2 = st
    pick = lambda a, b: jnp.where(complete, a, b)

    st = lax.while_loop(searching, functools.partial(search_pass, count_ge),
                        (p2, pick(lo2, lo1), pick(hi2, hi1), pick(c_lo2, c_lo1), pick(c_hi2, c_hi1),
                         pick(done2, done1)))
    _, lo, _, c_lo, c_hi, _ = st
    thr = _key_to_float(lo)

    need = jnp.where(c_lo > n_sel, n_sel - c_hi, float(seq))
    rt = RANK_TILE
    earlier = (lax.broadcasted_iota(jnp.int32, (rt, rt), 1)
               < lax.broadcasted_iota(jnp.int32, (rt, rt), 0)).astype(jnp.bfloat16)

    def mask_chunk(c, ties_before):
        for t0 in range(0, ck, rt):
            c0 = pl.multiple_of(c * ck + t0, rt)
            tie = jnp.where(s_ref[pl.ds(c0, rt), :] == thr, 1.0, 0.0)
            rank = ties_before + jnp.dot(earlier, tie.astype(jnp.bfloat16),
                                         preferred_element_type=jnp.float32)
            for r in range(0, rt, SEL_ROWS):
                r0 = pl.multiple_of(c0 + r, SEL_ROWS)
                x = s_ref[pl.ds(r0, SEL_ROWS), :]
                kept_tie = jnp.logical_and(x == thr, rank[r:r + SEL_ROWS] < need)
                keep = jnp.logical_and(jnp.logical_or(x > thr, kept_tie), r0 + key_iota <= q_pos)
                mask_ref[:, pl.ds(r0, SEL_ROWS)] = jnp.where(keep, 0.0, NEG_MASK).T.astype(jnp.bfloat16)
            ties_before = ties_before + jnp.sum(_fold_rows(tie, jnp.add), axis=0, keepdims=True)
        return ties_before

    lax.fori_loop(0, n_chunks, mask_chunk, jnp.zeros((1, tq), jnp.float32))

    def fill_chunk(c, _):
        r0 = pl.multiple_of(c * ck, ck)
        mask_ref[:, pl.ds(r0, ck)] = jnp.full((tq, ck), NEG_MASK, jnp.bfloat16)
        return 0

    lax.fori_loop(n_chunks, seq // ck, fill_chunk, 0)


def _select(k_idx, qi_t, w_t, n_sel):
    s = k_idx.shape[0]
    tq = SEL_TILE
    assert n_sel <= RANK_TILE and s % SEL_CHUNK == 0 and SEL_CHUNK % RANK_TILE == 0
    return pl.pallas_call(
        functools.partial(_select_kernel, seq=s, n_sel=n_sel),
        grid=(s // tq,),
        in_specs=[pl.BlockSpec(k_idx.shape, lambda i: (0, 0)),
                  pl.BlockSpec((qi_t.shape[0], tq), lambda i: (0, i)),
                  pl.BlockSpec((IDX_HEADS, tq), lambda i: (0, i))],
        out_specs=pl.BlockSpec((tq, s), lambda i: (i, 0)),
        out_shape=jax.ShapeDtypeStruct((s, s), jnp.bfloat16),
        scratch_shapes=[pltpu.VMEM((s, tq), jnp.float32), pltpu.VMEM((RANK_TILE, tq), jnp.float32),
                        pltpu.VMEM((2 * SEL_CHUNK, tq), jnp.float32)],
        compiler_params=_cparams(("parallel",)),
        name="dsa_select",
    )(k_idx, qi_t, w_t)


def _qk(a, b):
    return lax.dot_general(a, b, (((1,), (1,)), ((), ())), preferred_element_type=jnp.float32)


def _pair_tables(nq):
    qt = np.concatenate([np.full(q + 1, q, np.int32) for q in range(nq)])
    st = np.concatenate([np.arange(q + 1, dtype=np.int32) for q in range(nq)])
    return jnp.asarray(qt), jnp.asarray(st)


def _softmax_step(h, s, v, m_ref, acc_ref):
    ts = s.shape[1]
    m_prev = m_ref[h]
    m_new = jnp.maximum(m_prev, jnp.max(s, axis=-1, keepdims=True))
    alpha = jnp.exp2(m_prev - m_new)
    pm = jnp.exp2(s - jnp.tile(m_new, (1, ts // LANES))).astype(jnp.bfloat16)
    v1 = jnp.concatenate([v, jnp.ones((ts, HEAD_DIM), jnp.bfloat16)], axis=1)
    acc_ref[h] = (jnp.tile(alpha, (1, 2 * HEAD_DIM // LANES)) * acc_ref[h]
                  + jnp.dot(pm, v1, preferred_element_type=jnp.float32))
    m_ref[h] = m_new


def _init_softmax(m_ref, acc_ref):
    m_ref[...] = jnp.full(m_ref.shape, NEG_MASK, jnp.float32)
    acc_ref[...] = jnp.zeros(acc_ref.shape, jnp.float32)


def _softmax_result(h, acc_ref):
    return acc_ref[h, :, :HEAD_DIM] / acc_ref[h, :, HEAD_DIM:]


def _att_scratch(tq):
    return [pltpu.VMEM((N_HEADS, tq, LANES), jnp.float32),
            pltpu.VMEM((N_HEADS, tq, 2 * HEAD_DIM), jnp.float32)]


def _dsa_kernel(qt_ref, st_ref, q_ref, k_ref, v_ref, mask_ref, bias_ref, o_ref, m_ref, acc_ref):
    p = pl.program_id(0)
    qi, si = qt_ref[p], st_ref[p]
    near = qi - si < 2
    diag = qi == si
    zero, lag0, lag1 = 2, 0, 1
    quad = [[jnp.where(diag, lag0, zero), jnp.where(diag, zero, lag1)],
            [jnp.where(diag, lag1, zero), jnp.where(diag, lag0, zero)]]

    @pl.when(si == 0)
    def _():
        _init_softmax(m_ref, acc_ref)

    def sweep(with_bias):
        maskf = mask_ref[...].astype(jnp.float32)
        for h in range(N_HEADS):
            hs = slice(h * HEAD_DIM, (h + 1) * HEAD_DIM)
            s = _qk(q_ref[:, hs], k_ref[:, hs]) + maskf
            if with_bias:
                s = s + jnp.concatenate(
                    [jnp.concatenate([bias_ref[h, quad[a][b]] for b in range(2)], axis=1)
                     for a in range(2)], axis=0)
            _softmax_step(h, s, v_ref[:, hs], m_ref, acc_ref)

    pl.when(near)(functools.partial(sweep, True))
    pl.when(jnp.logical_not(near))(functools.partial(sweep, False))

    @pl.when(si == qi)
    def _():
        for h in range(N_HEADS):
            hs = slice(h * HEAD_DIM, (h + 1) * HEAD_DIM)
            o_ref[:, hs] = _softmax_result(h, acc_ref).astype(o_ref.dtype)


def _dsa(proj, mask, bias_tiles):
    s = proj.shape[0]
    t = ATT_TILE
    assert t == 2 * BIAS_TILE and bias_tiles.shape[1:] == (3, BIAS_TILE, BIAS_TILE)
    qt, st = _pair_tables(s // t)
    w = GROUP_WIDTH
    grid_spec = pltpu.PrefetchScalarGridSpec(
        num_scalar_prefetch=2,
        grid=(qt.shape[0],),
        in_specs=[pl.BlockSpec((t, w), lambda p, qt, st: (qt[p], 0)),
                  pl.BlockSpec((t, w), lambda p, qt, st: (st[p], 1)),
                  pl.BlockSpec((t, w), lambda p, qt, st: (st[p], 2)),
                  pl.BlockSpec((t, t), lambda p, qt, st: (qt[p], st[p])),
                  pl.BlockSpec(bias_tiles.shape, lambda p, qt, st: (0, 0, 0, 0))],
        out_specs=pl.BlockSpec((t, w), lambda p, qt, st: (qt[p], 0)),
        scratch_shapes=_att_scratch(t))
    return pl.pallas_call(
        _dsa_kernel, grid_spec=grid_spec,
        out_shape=jax.ShapeDtypeStruct((s, w), jnp.bfloat16),
        compiler_params=_cparams(("arbitrary",)),
        name="dsa_attention",
    )(qt, st, proj, proj, proj, mask, bias_tiles)


def _fox_kernel(qt_ref, st_ref, se_ref, live_ref, q_ref, k_ref, v_ref, g_ref, fq_ref, fs_ref, o_ref,
                m_ref, acc_ref):
    p = pl.program_id(0)
    qi, si = qt_ref[p], st_ref[p]
    tq, ts = q_ref.shape[0], k_ref.shape[0]

    @pl.when(si == 0)
    def _():
        _init_softmax(m_ref, acc_ref)

    def sweep(diagonal):
        if diagonal:
            causal = (lax.broadcasted_iota(jnp.int32, (tq, ts), 1)
                      <= lax.broadcasted_iota(jnp.int32, (tq, ts), 0))
        for h in range(N_HEADS):
            hs = slice(h * HEAD_DIM, (h + 1) * HEAD_DIM)
            decay = (fq_ref[h:h + 1, 0:1] - fs_ref[h:h + 1, :]) * LOG2E
            s = _qk(q_ref[:, hs], k_ref[:, hs]) + decay
            if diagonal:
                s = jnp.where(causal, s, NEG_MASK)
            _softmax_step(h, s, v_ref[:, hs], m_ref, acc_ref)

    pl.when(si == qi)(functools.partial(sweep, True))
    pl.when(jnp.logical_and(si != qi, live_ref[p] != 0))(functools.partial(sweep, False))

    @pl.when(si == qi)
    def _():
        for h in range(N_HEADS):
            hs = slice(h * HEAD_DIM, (h + 1) * HEAD_DIM)
            g = g_ref[:, hs].astype(jnp.float32)
            o_ref[:, hs] = (_softmax_result(h, acc_ref) / (1.0 + jnp.exp(-g))).astype(o_ref.dtype)


def _fox_schedule(proj, f_rows, qt, st, t):
    n = proj.shape[0] // t
    gw = GROUP_WIDTH

    def tile_norm(col0):
        x = proj[:, col0:col0 + gw].astype(jnp.float32).reshape(n, t, N_HEADS, HEAD_DIM)
        return jnp.sqrt(jnp.max(jnp.sum(x * x, axis=-1), axis=1)).T

    qn, kn = tile_norm(3 * gw), tile_norm(4 * gw)
    f = f_rows.reshape(N_HEADS, n, t)
    f_min, f_max = jnp.min(f, axis=-1), jnp.max(f, axis=-1)
    bound = (NORM_SLACK * qn[:, :, None] * (kn[:, None, :] + kn[:, :, None])
             - (f_min[:, None, :] - f_max[:, :, None]) * LOG2E)
    live_tiles = jnp.any(bound > -SKIP_LOG2, axis=0)
    live = jnp.logical_or(live_tiles[qt, st], qt == st)
    steps = jnp.arange(qt.shape[0], dtype=jnp.int32)
    next_live = lax.cummin(jnp.where(live, steps, qt.shape[0]), axis=0, reverse=True)
    return st[next_live], live.astype(jnp.int32)


def _fox(proj, f_rows):
    s = proj.shape[0]
    t = ATT_TILE
    qt, st = _pair_tables(s // t)
    se, live = _fox_schedule(proj, f_rows, qt, st, t)
    w = GROUP_WIDTH
    grid_spec = pltpu.PrefetchScalarGridSpec(
        num_scalar_prefetch=4,
        grid=(qt.shape[0],),
        in_specs=[pl.BlockSpec((t, w), lambda p, qt, st, se, lv: (qt[p], 3)),
                  pl.BlockSpec((t, w), lambda p, qt, st, se, lv: (se[p], 4)),
                  pl.BlockSpec((t, w), lambda p, qt, st, se, lv: (se[p], 5)),
                  pl.BlockSpec((t, w), lambda p, qt, st, se, lv: (qt[p], 6)),
                  pl.BlockSpec((N_HEADS, t), lambda p, qt, st, se, lv: (0, qt[p])),
                  pl.BlockSpec((N_HEADS, t), lambda p, qt, st, se, lv: (0, se[p]))],
        out_specs=pl.BlockSpec((t, w), lambda p, qt, st, se, lv: (qt[p], 0)),
        scratch_shapes=_att_scratch(t))
    return pl.pallas_call(
        _fox_kernel, grid_spec=grid_spec,
        out_shape=jax.ShapeDtypeStruct((s, w), jnp.bfloat16),
        compiler_params=_cparams(("arbitrary",)),
        name="fox_attention",
    )(qt, st, se, live, proj, proj, proj, proj, f_rows, f_rows)


def _outproj_kernel(x_ref, oa_ref, ob_ref, wa_ref, wb_ref, gate_ref, o_ref):
    y = jnp.dot(oa_ref[...], wa_ref[...], preferred_element_type=jnp.float32)
    y = y + jnp.dot(ob_ref[...], wb_ref[...], preferred_element_type=jnp.float32)
    o_ref[...] = x_ref[...] + gate_ref[...] * y


def _outproj(x, o_a, o_b, w_a, w_b, gate):
    s, d = x.shape
    tm, tn = ROW_TILE, 1024
    w = GROUP_WIDTH
    return pl.pallas_call(
        _outproj_kernel,
        grid=(s // tm, d // tn),
        in_specs=[pl.BlockSpec((tm, tn), lambda i, j: (i, j)),
                  pl.BlockSpec((tm, w), lambda i, j: (i, 0)),
                  pl.BlockSpec((tm, w), lambda i, j: (i, 0)),
                  pl.BlockSpec((w, tn), lambda i, j: (0, j)),
                  pl.BlockSpec((w, tn), lambda i, j: (0, j)),
                  pl.BlockSpec((1, tn), lambda i, j: (0, j))],
        out_specs=pl.BlockSpec((tm, tn), lambda i, j: (i, j)),
        out_shape=jax.ShapeDtypeStruct((s, d), jnp.float32),
        compiler_params=_cparams(("parallel", "arbitrary")),
        name="outproj",
    )(x, o_a, o_b, w_a, w_b, gate)


def _conv(y_ref, cw_ref, cb_ref, tm):
    h = CONV_HALO
    return (cw_ref[2:3, :] * y_ref[h:h + tm, :] + cw_ref[1:2, :] * y_ref[h - 1:h - 1 + tm, :]
            + cw_ref[0:1, :] * y_ref[h - 2:h - 2 + tm, :] + cb_ref[...])


def _ffn_kernel(x_ref, xh_ref, g_ref, sh_ref, sc_ref, wg_ref, wv_ref, cwg_ref, cwv_ref, cbg_ref, cbv_ref,
                wd_ref, gate_ref, gf_ref, o_ref, h_ref, yg_ref, yv_ref, acc_ref):
    i, f = pl.program_id(0), pl.program_id(1)
    tm = x_ref.shape[0]

    @pl.when(f == 0)
    def _():
        halo = _norm_mod(xh_ref[...], g_ref[...], sh_ref[...], sc_ref[...])
        h_ref[0:CONV_HALO, :] = jnp.where(i > 0, halo, 0.0).astype(jnp.bfloat16)
        h_ref[CONV_HALO:, :] = _norm_mod(x_ref[...], g_ref[...], sh_ref[...], sc_ref[...]).astype(jnp.bfloat16)
        acc_ref[...] = jnp.zeros(acc_ref.shape, jnp.float32)

    hb = h_ref[...]
    yg_ref[...] = jnp.dot(hb, wg_ref[...], preferred_element_type=jnp.float32)
    yv_ref[...] = jnp.dot(hb, wv_ref[...], preferred_element_type=jnp.float32)
    ug = _conv(yg_ref, cwg_ref, cbg_ref, tm)
    uv = _conv(yv_ref, cwv_ref, cbv_ref, tm)
    a = (ug / (1.0 + jnp.exp(-ug))) * uv
    acc_ref[...] += jnp.dot(a.astype(jnp.bfloat16), wd_ref[...], preferred_element_type=jnp.float32)

    @pl.when(f == pl.num_programs(1) - 1)
    def _():
        x2 = x_ref[...] + gate_ref[...] * acc_ref[...]
        y = x2 * lax.rsqrt(jnp.mean(x2 * x2, axis=-1, keepdims=True) + EPS)
        o_ref[...] = y * gf_ref[...]


def _ffn(x, g, shift, scale, w_up, conv_w, conv_b, w_down, gate, g_final):
    s, d = x.shape
    fp = w_down.shape[0]
    tm, tf = ROW_TILE, FF_TILE
    nf = fp // tf
    hb = tm // CONV_HALO
    row = lambda i, f: (0, 0)
    return pl.pallas_call(
        _ffn_kernel,
        grid=(s // tm, nf),
        in_specs=[pl.BlockSpec((tm, d), lambda i, f: (i, 0)),
                  pl.BlockSpec((CONV_HALO, d), lambda i, f: (jnp.maximum(i * hb - 1, 0), 0)),
                  pl.BlockSpec((1, d), row), pl.BlockSpec((1, d), row), pl.BlockSpec((1, d), row),
                  pl.BlockSpec((d, tf), lambda i, f: (0, f)),
                  pl.BlockSpec((d, tf), lambda i, f: (0, f + nf)),
                  pl.BlockSpec((CONV_WIDTH, tf), lambda i, f: (0, f)),
                  pl.BlockSpec((CONV_WIDTH, tf), lambda i, f: (0, f + nf)),
                  pl.BlockSpec((1, tf), lambda i, f: (0, f)),
                  pl.BlockSpec((1, tf), lambda i, f: (0, f + nf)),
                  pl.BlockSpec((tf, d), lambda i, f: (f, 0)),
                  pl.BlockSpec((1, d), row), pl.BlockSpec((1, d), row)],
        out_specs=pl.BlockSpec((tm, d), lambda i, f: (i, 0)),
        out_shape=jax.ShapeDtypeStruct((s, d), jnp.float32),
        scratch_shapes=[pltpu.VMEM((tm + CONV_HALO, d), jnp.bfloat16),
                        pltpu.VMEM((tm + CONV_HALO, tf), jnp.float32),
                        pltpu.VMEM((tm + CONV_HALO, tf), jnp.float32),
                        pltpu.VMEM((tm, d), jnp.float32)],
        compiler_params=_cparams(("parallel", "arbitrary")),
        name="conv_ffn",
    )(x, x, g, shift, scale, w_up, w_up, conv_w, conv_w, conv_b, conv_b, w_down, gate, g_final)


def _t5_bias_tiles(rel_bias, t):
    max_exact = N_BUCKETS // 2
    d = jnp.arange(2 * t, dtype=jnp.int32)
    df = jnp.maximum(d, 1).astype(jnp.float32)
    large = max_exact + (jnp.log(df / max_exact) / math.log(MAX_DISTANCE / max_exact)
                         * (N_BUCKETS - max_exact)).astype(jnp.int32)
    bucket = jnp.where(d < max_exact, d, jnp.minimum(large, N_BUCKETS - 1))
    rb = rel_bias.astype(jnp.float32)
    table = (rb[bucket] - rb[N_BUCKETS - 1]) * LOG2E
    period = 2 * t
    k = np.arange(period)
    tiles = []
    for off in (0, t):
        idx = np.clip(np.where(k < t, off - k, off + period - k), 0, 2 * t - 1)
        seq = jnp.tile(table[idx].T, (1, t))
        tiles.append(seq[:, :t * (period - 1)].reshape(N_HEADS, t, period - 1)[:, :, :t])
    return jnp.stack(tiles + [jnp.zeros_like(tiles[0])], axis=1)


def _pad_cols(a, n):
    return jnp.pad(a, ((0, 0), (0, n - a.shape[1])))


def kernel(x, c, rel_bias, w_ada, b_ada, g_attn, w_in, b_forget, w_out, g_mlp, w_up, conv_w, conv_b,
           w_down, g_final):
    b, s, d = x.shape
    assert b == 1 and s % ROW_TILE == 0 and s % ATT_TILE == 0 and s % SEL_TILE == 0
    assert BIAS_TILE >= MAX_DISTANCE
    depth = w_ada.shape[0]
    assert depth == 1, "the fused MLP kernel applies the final norm, so it must be the last layer"
    d_ff = w_down.shape[1]
    ff_pad = -(-d_ff // FF_TILE) * FF_TILE
    n_sel = min(TOPK_MAX, s // 4)
    gw, nq = GROUP_WIDTH, IDX_HEADS * IDX_DIM
    bf = jnp.bfloat16
    xs = x[0]
    c_col = c.reshape(d, 1)
    bias_tiles = _t5_bias_tiles(rel_bias, BIAS_TILE)

    for l in range(depth):
        mod = _ada(c_col, w_ada[l], b_ada[l][None, :])
        shift_a, scale_a, gate_a, shift_m, scale_m, gate_m = [mod[:, k * d:(k + 1) * d] for k in range(6)]

        wi = w_in[l]
        o = np.cumsum([0, gw, gw, gw, nq, IDX_DIM, IDX_HEADS, gw, gw, gw, gw, N_HEADS])
        seg = lambda k: wi[:, o[k]:o[k + 1]]
        q_scale = HEAD_DIM ** -0.5 * LOG2E
        w_main = jnp.concatenate([seg(0) * q_scale, seg(1), seg(2),
                                  seg(6) * q_scale, seg(7), seg(8), seg(9)], axis=1).astype(bf)
        w_small = _pad_cols(jnp.concatenate([seg(3) * IDX_DIM ** -0.5, seg(4), seg(5), seg(10)], axis=1),
                            SMALL_WIDTH).astype(bf)

        proj, qi, small = _inproj(xs, g_attn[l][None, :], shift_a, scale_a, w_main, w_small)

        k_idx = small[:, :IDX_DIM].astype(bf)
        w_t = small[:, IDX_DIM:IDX_DIM + IDX_HEADS].T
        fb_t = small[:, IDX_DIM + IDX_HEADS:IDX_DIM + IDX_HEADS + N_HEADS].T
        f_rows = _fcum(fb_t.reshape(N_HEADS, s // LANES, LANES), b_forget[l]).reshape(N_HEADS, s)

        mask = _select(k_idx, qi.T, w_t, n_sel)
        o_a = _dsa(proj, mask, bias_tiles)
        o_b = _fox(proj, f_rows)

        wo = w_out[l].astype(bf)
        x1 = _outproj(xs, o_a, o_b, wo[:gw], wo[gw:], gate_a)

        wu = w_up[l]
        w_up_p = jnp.concatenate([_pad_cols(wu[:, :d_ff], ff_pad), _pad_cols(wu[:, d_ff:], ff_pad)], axis=1).astype(bf)
        cw = conv_w[l]
        cw_p = jnp.concatenate([_pad_cols(cw[:, :d_ff], ff_pad), _pad_cols(cw[:, d_ff:], ff_pad)], axis=1)
        cb = conv_b[l][None, :]
        cb_p = jnp.concatenate([_pad_cols(cb[:, :d_ff], ff_pad), _pad_cols(cb[:, d_ff:], ff_pad)], axis=1)
        w_down_p = jnp.pad(w_down[l], ((0, ff_pad - d_ff), (0, 0))).astype(bf)

        xs = _ffn(x1, g_mlp[l][None, :], shift_m, scale_m, w_up_p, cw_p, cb_p, w_down_p, gate_m,
                  g_final[None, :])

    return xs[None]
```

```python
import functools
import math

import jax
import jax.numpy as jnp
import numpy as np
from jax import lax
from jax.experimental import pallas as pl
from jax.experimental.pallas import tpu as pltpu

HEAD_DIM = 128
N_HEADS = 8
GROUP_WIDTH = N_HEADS * HEAD_DIM
IDX_HEADS = 8
IDX_DIM = 64
TOPK_MAX = 256
N_BUCKETS = 32
MAX_DISTANCE = 128
CONV_WIDTH = 3
EPS = 1e-6
NEG = -1e30
NEG_MASK = -(2.0 ** 100)

LANES = 128
SUBLANES = 8
VMEM_LIMIT = 56 * 1024 * 1024

ROW_TILE = 512
ATT_TILE = 512
DSA_Q_TILE = 1024
SEL_TILE = 256
BIAS_TILE = 256
LOG2E = math.log2(math.e)
SEL_ROWS = 128
SEL_CHUNK = 512
COUNT_ROWS = 64
SHORTLIST = 12
SKIP_LOG2 = 160.0
NORM_SLACK = 1.01
RANK_TILE = 256
FF_TILE = 512
CONV_HALO = 16
SMALL_WIDTH = 640


def _cparams(sem):
    return pltpu.CompilerParams(dimension_semantics=sem, vmem_limit_bytes=VMEM_LIMIT)


def _ada_kernel(c_ref, w_ref, b_ref, o_ref):
    d = w_ref.shape[0]
    rows = 256

    def body(r, acc):
        sl = pl.ds(pl.multiple_of(r * rows, rows), rows)
        cc = c_ref[sl, :]
        ca = cc / (1.0 + jnp.exp(-cc))
        prod = w_ref[sl, :] * ca
        return acc + prod.reshape(rows // SUBLANES, SUBLANES, -1).sum(axis=0)

    acc = lax.fori_loop(0, d // rows, body, jnp.zeros((SUBLANES, w_ref.shape[1]), jnp.float32))
    o_ref[...] = jnp.sum(acc, axis=0, keepdims=True) + b_ref[...]


def _ada(c_col, w, b_row):
    d, n = w.shape
    tn = 1024
    return pl.pallas_call(
        _ada_kernel,
        grid=(n // tn,),
        in_specs=[pl.BlockSpec((d, 1), lambda j: (0, 0)),
                  pl.BlockSpec((d, tn), lambda j: (0, j)),
                  pl.BlockSpec((1, tn), lambda j: (0, j))],
        out_specs=pl.BlockSpec((1, tn), lambda j: (0, j)),
        out_shape=jax.ShapeDtypeStruct((1, n), jnp.float32),
        compiler_params=_cparams(("arbitrary",)),
        name="adaln",
    )(c_col, w, b_row)


def _norm_mod(x, g, shift, scale):
    y = x * lax.rsqrt(jnp.mean(x * x, axis=-1, keepdims=True) + EPS)
    return (y * g) * (1.0 + scale) + shift


def _inproj_kernel(x_ref, g_ref, sh_ref, sc_ref, wm_ref, ws_ref, om_ref, oq_ref, os_ref, h_ref):
    j = pl.program_id(1)

    @pl.when(j == 0)
    def _():
        h = _norm_mod(x_ref[...], g_ref[...], sh_ref[...], sc_ref[...]).astype(jnp.bfloat16)
        h_ref[...] = h
        small = jnp.dot(h, ws_ref[...], preferred_element_type=jnp.float32)
        nq = IDX_HEADS * IDX_DIM
        oq_ref[...] = small[:, :nq].astype(jnp.bfloat16)
        os_ref[...] = small[:, nq:]

    om_ref[...] = jnp.dot(h_ref[...], wm_ref[...],
                          preferred_element_type=jnp.float32).astype(jnp.bfloat16)


def _inproj(x, g, shift, scale, w_main, w_small):
    s, d = x.shape
    n_main = w_main.shape[1]
    tm, tn = ROW_TILE, GROUP_WIDTH
    nq = IDX_HEADS * IDX_DIM
    row = lambda i, j: (0, 0)
    return pl.pallas_call(
        _inproj_kernel,
        grid=(s // tm, n_main // tn),
        in_specs=[pl.BlockSpec((tm, d), lambda i, j: (i, 0)),
                  pl.BlockSpec((1, d), row), pl.BlockSpec((1, d), row), pl.BlockSpec((1, d), row),
                  pl.BlockSpec((d, tn), lambda i, j: (0, j)),
                  pl.BlockSpec((d, SMALL_WIDTH), row)],
        out_specs=[pl.BlockSpec((tm, tn), lambda i, j: (i, j)),
                   pl.BlockSpec((tm, nq), lambda i, j: (i, 0)),
                   pl.BlockSpec((tm, SMALL_WIDTH - nq), lambda i, j: (i, 0))],
        out_shape=[jax.ShapeDtypeStruct((s, n_main), jnp.bfloat16),
                   jax.ShapeDtypeStruct((s, nq), jnp.bfloat16),
                   jax.ShapeDtypeStruct((s, SMALL_WIDTH - nq), jnp.float32)],
        scratch_shapes=[pltpu.VMEM((tm, d), jnp.bfloat16)],
        compiler_params=_cparams(("parallel", "arbitrary")),
        name="inproj",
    )(x, g, shift, scale, w_main, w_small)


def _fcum_kernel(fb_ref, b_ref, o_ref):
    r = fb_ref.shape[1]
    hi = lax.Precision.HIGHEST
    ci = lax.broadcasted_iota(jnp.int32, (LANES, LANES), 0)
    cj = lax.broadcasted_iota(jnp.int32, (LANES, LANES), 1)
    upper = (ci <= cj).astype(jnp.float32)
    ri = lax.broadcasted_iota(jnp.int32, (r, r), 0)
    rj = lax.broadcasted_iota(jnp.int32, (r, r), 1)
    strict_lower = (rj < ri).astype(jnp.float32)
    for h in range(N_HEADS):
        z = fb_ref[h] + b_ref[h]
        lf = jnp.minimum(z, 0.0) - jnp.log(1.0 + jnp.exp(-jnp.abs(z)))
        within = jnp.dot(lf, upper, precision=hi, preferred_element_type=jnp.float32)
        before = jnp.dot(strict_lower, within, precision=hi, preferred_element_type=jnp.float32)
        o_ref[h] = within + before[:, LANES - 1:LANES]


def _fcum(fb_t, b_forget):
    h, r, _ = fb_t.shape
    return pl.pallas_call(
        _fcum_kernel,
        in_specs=[pl.BlockSpec(memory_space=pltpu.VMEM),
                  pl.BlockSpec(memory_space=pltpu.SMEM)],
        out_specs=pl.BlockSpec(memory_space=pltpu.VMEM),
        out_shape=jax.ShapeDtypeStruct((h, r, LANES), jnp.float32),
        name="forget_cumsum",
    )(fb_t, b_forget)


def _ordered_bits(v):
    return v ^ ((v >> 31) & jnp.int32(0x7FFFFFFF))


def _float_to_key(x):
    return _ordered_bits(pltpu.bitcast(x, jnp.int32))


def _key_to_float(key):
    return pltpu.bitcast(_ordered_bits(key), jnp.float32)


def _fold_rows(x, op):
    parts = [x[r:r + SUBLANES] for r in range(0, x.shape[0], SUBLANES)]
    while len(parts) > 1:
        parts = [op(parts[k], parts[k + 1]) for k in range(0, len(parts) - 1, 2)] + (
            [parts[-1]] if len(parts) % 2 else [])
    return parts[0]


def _select_kernel(k_ref, qit_ref, wt_ref, mask_ref, s_ref, top_ref, short_ref, *, seq, n_sel):
    tq = qit_ref.shape[1]
    ck = SEL_CHUNK
    i = pl.program_id(0)
    n_chunks = ((i + 1) * tq + ck - 1) // ck
    n_rest = (seq - n_chunks * ck).astype(jnp.float32)
    q_pos = i * tq + lax.broadcasted_iota(jnp.int32, (SEL_ROWS, tq), 1)
    key_iota = lax.broadcasted_iota(jnp.int32, (SEL_ROWS, tq), 0)
    wts = wt_ref[...] * (IDX_HEADS ** -0.5)
    n_slots = top_ref.shape[0]

    top_ref[...] = jnp.full(top_ref.shape, -jnp.inf, jnp.float32)

    def score_chunk(c, signs):
        pos, nonneg = signs
        for r in range(0, ck, SEL_ROWS):
            r0 = pl.multiple_of(c * ck + r, SEL_ROWS)
            kc = k_ref[pl.ds(r0, SEL_ROWS), :]
            acc = jnp.zeros((SEL_ROWS, tq), jnp.float32)
            for h in range(IDX_HEADS):
                lg = jnp.dot(kc, qit_ref[h * IDX_DIM:(h + 1) * IDX_DIM, :],
                             preferred_element_type=jnp.float32)
                acc = acc + jnp.maximum(lg, 0.0) * wts[h:h + 1, :]
            sc = jnp.where(r0 + key_iota <= q_pos, acc, NEG)
            s_ref[pl.ds(r0, SEL_ROWS), :] = sc
            slot = r % n_slots
            top_ref[slot:slot + SEL_ROWS, :] = jnp.maximum(top_ref[slot:slot + SEL_ROWS, :], sc)
            pos = pos + _fold_rows(jnp.where(sc > 0.0, 1.0, 0.0), jnp.add)
            nonneg = nonneg + _fold_rows(jnp.where(sc >= 0.0, 1.0, 0.0), jnp.add)
        return pos, nonneg

    zero_count = jnp.zeros((SUBLANES, tq), jnp.float32)
    pos, nonneg = lax.fori_loop(0, n_chunks, score_chunk, (zero_count, zero_count))
    n_pos = jnp.sum(pos, axis=0, keepdims=True)
    n_nonneg = jnp.sum(nonneg, axis=0, keepdims=True)

    def count(pred, ref=s_ref, chunks=n_chunks):
        def body(c, accs):
            for r in range(0, ck, COUNT_ROWS):
                x = ref[pl.ds(pl.multiple_of(c * ck + r, COUNT_ROWS), COUNT_ROWS), :]
                accs = tuple(a + _fold_rows(jnp.where(f(x), 1.0, 0.0), jnp.add) for a, f in zip(accs, pred))
            return accs
        zero = jnp.zeros((SUBLANES, tq), jnp.float32)
        accs = lax.fori_loop(0, chunks, body, (zero,) * len(pred))
        return [jnp.sum(a, axis=0, keepdims=True) for a in accs]

    def count_ge(cf):
        return count([lambda x: x >= cf])[0] + jnp.where(NEG >= cf, n_rest, 0.0)

    top = top_ref[...]
    low_key = _float_to_key(jnp.min(_fold_rows(top, jnp.minimum), axis=0, keepdims=True))
    max_key = _float_to_key(jnp.max(_fold_rows(top, jnp.maximum), axis=0, keepdims=True))
    hi0 = jnp.maximum(max_key, _float_to_key(jnp.full((1, tq), NEG, jnp.float32))) + 1
    lo0 = jnp.full((1, tq), -2 ** 31, jnp.int32)
    max_passes = 96

    at_zero = jnp.logical_and(n_pos < n_sel, n_nonneg >= n_sel)
    above_zero = n_pos >= n_sel
    below_zero = n_nonneg < n_sel
    zero_key = jnp.zeros((1, tq), jnp.int32)
    lo0 = jnp.where(below_zero, lo0, zero_key)
    c_lo0 = jnp.where(below_zero, float(seq), n_nonneg)
    hi0 = jnp.where(above_zero, hi0, jnp.where(at_zero, zero_key + 2 ** 23, zero_key - 1))
    c_hi0 = jnp.where(above_zero, 0.0, jnp.where(at_zero, n_pos, n_nonneg))
    done0 = jnp.where(at_zero, 1, 0)

    def searching(st):
        p, _, _, _, _, done = st
        return jnp.logical_and(p < max_passes, jnp.min(done) == 0)

    def search_pass(count_fn, st):
        p, lo, hi, c_lo, c_hi, done = st
        mid_val = _float_to_key(0.5 * _key_to_float(lo) + 0.5 * _key_to_float(hi))
        mid_key = (lo >> 1) + (hi >> 1) + (lo & hi & 1)
        cand = jnp.where(p == 0, low_key, jnp.where((p & 3) == 0, mid_key, mid_val))
        cand = jnp.minimum(jnp.maximum(cand, lo + 1), hi - 1)
        cnt = count_fn(_key_to_float(cand))
        active = done == 0
        up = jnp.logical_and(active, cnt >= n_sel)
        down = jnp.logical_and(active, cnt < n_sel)
        lo, c_lo = jnp.where(up, cand, lo), jnp.where(up, cnt, c_lo)
        hi, c_hi = jnp.where(down, cand, hi), jnp.where(down, cnt, c_hi)
        finished = jnp.logical_and(active, jnp.logical_or(cnt == n_sel, hi - 1 <= lo))
        return p + 1, lo, hi, c_lo, c_hi, jnp.where(finished, 1, done)

    def narrowing(st):
        p, _, _, c_lo, c_hi, done = st
        wide = jnp.logical_and(done == 0, c_lo - c_hi > SHORTLIST)
        return jnp.logical_and(p < max_passes, jnp.max(jnp.where(wide, 1, 0)) > 0)

    st = lax.while_loop(narrowing, functools.partial(search_pass, count_ge),
                        (jnp.int32(0), lo0, hi0, c_lo0, c_hi0, done0))

    p1, lo1, hi1, c_lo1, c_hi1, done1 = st
    hi_f = _key_to_float(hi1)
    short_ref[...] = jnp.full(short_ref.shape, -jnp.inf, jnp.float32)

    def shortlist_chunk(c, _):
        for r in range(0, ck, COUNT_ROWS):
            x = s_ref[pl.ds(pl.multiple_of(c * ck + r, COUNT_ROWS), COUNT_ROWS), :]
            y = jnp.where(x < hi_f, x, -jnp.inf)
            first = short_ref[r:r + COUNT_ROWS, :]
            second = short_ref[ck + r:ck + r + COUNT_ROWS, :]
            short_ref[r:r + COUNT_ROWS, :] = jnp.maximum(first, y)
            short_ref[ck + r:ck + r + COUNT_ROWS, :] = jnp.maximum(second, jnp.minimum(first, y))
        return 0

    lax.fori_loop(0, n_chunks, shortlist_chunk, 0)
    lo_f = _key_to_float(lo1)
    listed = count([lambda x: x >= lo_f], short_ref, short_ref.shape[0] // ck)[0]
    complete = listed == c_lo1 - c_hi1

    def count_short(cf):
        return c_hi1 + count([lambda x: x >= cf], short_ref, short_ref.shape[0] // ck)[0]

    st = lax.while_loop(searching, functools.partial(search_pass, count_short),
                        (p1, lo1, hi1, c_lo1, c_hi1, jnp.where(complete, done1, 1)))
    p2, lo2, hi2, c_lo2, c_hi2, done2 = st
    pick = lambda a, b: jnp.where(complete, a, b)

    st = lax.while_loop(searching, functools.partial(search_pass, count_ge),
                        (p2, pick(lo2, lo1), pick(hi2, hi1), pick(c_lo2, c_lo1), pick(c_hi2, c_hi1),
                         pick(done2, done1)))
    _, lo, _, c_lo, c_hi, _ = st
    thr = _key_to_float(lo)

    need = jnp.where(c_lo > n_sel, n_sel - c_hi, float(seq))
    rt = RANK_TILE
    earlier = (lax.broadcasted_iota(jnp.int32, (rt, rt), 1)
               < lax.broadcasted_iota(jnp.int32, (rt, rt), 0)).astype(jnp.bfloat16)

    def mask_chunk(c, ties_before):
        for t0 in range(0, ck, rt):
            c0 = pl.multiple_of(c * ck + t0, rt)
            tie = jnp.where(s_ref[pl.ds(c0, rt), :] == thr, 1.0, 0.0)
            rank = ties_before + jnp.dot(earlier, tie.astype(jnp.bfloat16),
                                         preferred_element_type=jnp.float32)
            for r in range(0, rt, SEL_ROWS):
                r0 = pl.multiple_of(c0 + r, SEL_ROWS)
                x = s_ref[pl.ds(r0, SEL_ROWS), :]
                kept_tie = jnp.logical_and(x == thr, rank[r:r + SEL_ROWS] < need)
                keep = jnp.logical_and(jnp.logical_or(x > thr, kept_tie), r0 + key_iota <= q_pos)
                mask_ref[:, pl.ds(r0, SEL_ROWS)] = jnp.where(keep, 0.0, NEG_MASK).T.astype(jnp.bfloat16)
            ties_before = ties_before + jnp.sum(_fold_rows(tie, jnp.add), axis=0, keepdims=True)
        return ties_before

    lax.fori_loop(0, n_chunks, mask_chunk, jnp.zeros((1, tq), jnp.float32))

    def fill_chunk(c, _):
        r0 = pl.multiple_of(c * ck, ck)
        mask_ref[:, pl.ds(r0, ck)] = jnp.full((tq, ck), NEG_MASK, jnp.bfloat16)
        return 0

    lax.fori_loop(n_chunks, seq // ck, fill_chunk, 0)


def _select(k_idx, qi_t, w_t, n_sel):
    s = k_idx.shape[0]
    tq = SEL_TILE
    assert n_sel <= RANK_TILE and s % SEL_CHUNK == 0 and SEL_CHUNK % RANK_TILE == 0
    return pl.pallas_call(
        functools.partial(_select_kernel, seq=s, n_sel=n_sel),
        grid=(s // tq,),
        in_specs=[pl.BlockSpec(k_idx.shape, lambda i: (0, 0)),
                  pl.BlockSpec((qi_t.shape[0], tq), lambda i: (0, i)),
                  pl.BlockSpec((IDX_HEADS, tq), lambda i: (0, i))],
        out_specs=pl.BlockSpec((tq, s), lambda i: (i, 0)),
        out_shape=jax.ShapeDtypeStruct((s, s), jnp.bfloat16),
        scratch_shapes=[pltpu.VMEM((s, tq), jnp.float32), pltpu.VMEM((RANK_TILE, tq), jnp.float32),
                        pltpu.VMEM((2 * SEL_CHUNK, tq), jnp.float32)],
        compiler_params=_cparams(("parallel",)),
        name="dsa_select",
    )(k_idx, qi_t, w_t)


def _qk(a, b):
    return lax.dot_general(a, b, (((1,), (1,)), ((), ())), preferred_element_type=jnp.float32)


def _pair_tables(nq):
    qt = np.concatenate([np.full(q + 1, q, np.int32) for q in range(nq)])
    st = np.concatenate([np.arange(q + 1, dtype=np.int32) for q in range(nq)])
    return jnp.asarray(qt), jnp.asarray(st)


def _softmax_step(h, s, v, m_ref, acc_ref):
    ts = s.shape[1]
    m_prev = m_ref[h]
    m_new = jnp.maximum(m_prev, jnp.max(s, axis=-1, keepdims=True))
    alpha = jnp.exp2(m_prev - m_new)
    pm = jnp.exp2(s - jnp.tile(m_new, (1, ts // LANES))).astype(jnp.bfloat16)
    v1 = jnp.concatenate([v, jnp.ones((ts, HEAD_DIM), jnp.bfloat16)], axis=1)
    acc_ref[h] = (jnp.tile(alpha, (1, 2 * HEAD_DIM // LANES)) * acc_ref[h]
                  + jnp.dot(pm, v1, preferred_element_type=jnp.float32))
    m_ref[h] = m_new


def _init_softmax(m_ref, acc_ref):
    m_ref[...] = jnp.full(m_ref.shape, NEG_MASK, jnp.float32)
    acc_ref[...] = jnp.zeros(acc_ref.shape, jnp.float32)


def _softmax_result(h, acc_ref):
    return acc_ref[h, :, :HEAD_DIM] / acc_ref[h, :, HEAD_DIM:]


def _att_scratch(tq):
    return [pltpu.VMEM((N_HEADS, tq, LANES), jnp.float32),
            pltpu.VMEM((N_HEADS, tq, 2 * HEAD_DIM), jnp.float32)]


def _dsa_kernel(qt_ref, st_ref, q_ref, k_ref, v_ref, mask_ref, bias_ref, o_ref, m_ref, acc_ref):
    p = pl.program_id(0)
    qi, si = qt_ref[p], st_ref[p]
    tq, ts = mask_ref.shape
    bt = bias_ref.shape[-1]
    lag00 = (qi * tq - si * ts) // bt
    lags = [[lag00 + a - b for b in range(ts // bt)] for a in range(tq // bt)]
    quad = [[jnp.where(lag == 0, 0, jnp.where(lag == 1, 1, 2)) for lag in row] for row in lags]
    near = lags[0][-1] <= 1
    last = (si + 1) * ts >= (qi + 1) * tq

    @pl.when(si == 0)
    def _():
        _init_softmax(m_ref, acc_ref)

    def sweep(with_bias):
        maskf = mask_ref[...].astype(jnp.float32)
        for h in range(N_HEADS):
            hs = slice(h * HEAD_DIM, (h + 1) * HEAD_DIM)
            s = _qk(q_ref[:, hs], k_ref[:, hs]) + maskf
            if with_bias:
                s = s + jnp.concatenate(
                    [jnp.concatenate([bias_ref[h, idx] for idx in row], axis=1) for row in quad], axis=0)
            _softmax_step(h, s, v_ref[:, hs], m_ref, acc_ref)

    pl.when(near)(functools.partial(sweep, True))
    pl.when(jnp.logical_not(near))(functools.partial(sweep, False))

    @pl.when(last)
    def _():
        for h in range(N_HEADS):
            hs = slice(h * HEAD_DIM, (h + 1) * HEAD_DIM)
            o_ref[:, hs] = _softmax_result(h, acc_ref).astype(o_ref.dtype)


def _dsa(proj, mask, bias_tiles):
    s = proj.shape[0]
    tq, ts = DSA_Q_TILE, ATT_TILE
    assert tq % ts == 0 and ts % BIAS_TILE == 0 and bias_tiles.shape[1:] == (3, BIAS_TILE, BIAS_TILE)
    r = tq // ts
    nq = s // tq
    qt = jnp.asarray(np.concatenate([np.full(r * (q + 1), q, np.int32) for q in range(nq)]))
    st = jnp.asarray(np.concatenate([np.arange(r * (q + 1), dtype=np.int32) for q in range(nq)]))
    w = GROUP_WIDTH
    grid_spec = pltpu.PrefetchScalarGridSpec(
        num_scalar_prefetch=2,
        grid=(qt.shape[0],),
        in_specs=[pl.BlockSpec((tq, w), lambda p, qt, st: (qt[p], 0)),
                  pl.BlockSpec((ts, w), lambda p, qt, st: (st[p], 1)),
                  pl.BlockSpec((ts, w), lambda p, qt, st: (st[p], 2)),
                  pl.BlockSpec((tq, ts), lambda p, qt, st: (qt[p], st[p])),
                  pl.BlockSpec(bias_tiles.shape, lambda p, qt, st: (0, 0, 0, 0))],
        out_specs=pl.BlockSpec((tq, w), lambda p, qt, st: (qt[p], 0)),
        scratch_shapes=_att_scratch(tq))
    return pl.pallas_call(
        _dsa_kernel, grid_spec=grid_spec,
        out_shape=jax.ShapeDtypeStruct((s, w), jnp.bfloat16),
        compiler_params=_cparams(("arbitrary",)),
        name="dsa_attention",
    )(qt, st, proj, proj, proj, mask, bias_tiles)


def _fox_kernel(qt_ref, st_ref, se_ref, live_ref, q_ref, k_ref, v_ref, g_ref, fq_ref, fs_ref, o_ref,
                m_ref, acc_ref):
    p = pl.program_id(0)
    qi, si = qt_ref[p], st_ref[p]
    tq, ts = q_ref.shape[0], k_ref.shape[0]

    @pl.when(si == 0)
    def _():
        _init_softmax(m_ref, acc_ref)

    def sweep(diagonal):
        if diagonal:
            causal = (lax.broadcasted_iota(jnp.int32, (tq, ts), 1)
                      <= lax.broadcasted_iota(jnp.int32, (tq, ts), 0))
        for h in range(N_HEADS):
            hs = slice(h * HEAD_DIM, (h + 1) * HEAD_DIM)
            decay = (fq_ref[h:h + 1, 0:1] - fs_ref[h:h + 1, :]) * LOG2E
            s = _qk(q_ref[:, hs], k_ref[:, hs]) + decay
            if diagonal:
                s = jnp.where(causal, s, NEG_MASK)
            _softmax_step(h, s, v_ref[:, hs], m_ref, acc_ref)

    pl.when(si == qi)(functools.partial(sweep, True))
    pl.when(jnp.logical_and(si != qi, live_ref[p] != 0))(functools.partial(sweep, False))

    @pl.when(si == qi)
    def _():
        for h in range(N_HEADS):
            hs = slice(h * HEAD_DIM, (h + 1) * HEAD_DIM)
            g = g_ref[:, hs].astype(jnp.float32)
            o_ref[:, hs] = (_softmax_result(h, acc_ref) / (1.0 + jnp.exp(-g))).astype(o_ref.dtype)


def _fox_schedule(proj, f_rows, qt, st, t):
    n = proj.shape[0] // t
    gw = GROUP_WIDTH

    def tile_norm(col0):
        x = proj[:, col0:col0 + gw].astype(jnp.float32).reshape(n, t, N_HEADS, HEAD_DIM)
        return jnp.sqrt(jnp.max(jnp.sum(x * x, axis=-1), axis=1)).T

    qn, kn = tile_norm(3 * gw), tile_norm(4 * gw)
    f = f_rows.reshape(N_HEADS, n, t)
    f_min, f_max = jnp.min(f, axis=-1), jnp.max(f, axis=-1)
    bound = (NORM_SLACK * qn[:, :, None] * (kn[:, None, :] + kn[:, :, None])
             - (f_min[:, None, :] - f_max[:, :, None]) * LOG2E)
    live_tiles = jnp.any(bound > -SKIP_LOG2, axis=0)
    live = jnp.logical_or(live_tiles[qt, st], qt == st)
    steps = jnp.arange(qt.shape[0], dtype=jnp.int32)
    next_live = lax.cummin(jnp.where(live, steps, qt.shape[0]), axis=0, reverse=True)
    return st[next_live], live.astype(jnp.int32)


def _fox(proj, f_rows):
    s = proj.shape[0]
    t = ATT_TILE
    qt, st = _pair_tables(s // t)
    se, live = _fox_schedule(proj, f_rows, qt, st, t)
    w = GROUP_WIDTH
    grid_spec = pltpu.PrefetchScalarGridSpec(
        num_scalar_prefetch=4,
        grid=(qt.shape[0],),
        in_specs=[pl.BlockSpec((t, w), lambda p, qt, st, se, lv: (qt[p], 3)),
                  pl.BlockSpec((t, w), lambda p, qt, st, se, lv: (se[p], 4)),
                  pl.BlockSpec((t, w), lambda p, qt, st, se, lv: (se[p], 5)),
                  pl.BlockSpec((t, w), lambda p, qt, st, se, lv: (qt[p], 6)),
                  pl.BlockSpec((N_HEADS, t), lambda p, qt, st, se, lv: (0, qt[p])),
                  pl.BlockSpec((N_HEADS, t), lambda p, qt, st, se, lv: (0, se[p]))],
        out_specs=pl.BlockSpec((t, w), lambda p, qt, st, se, lv: (qt[p], 0)),
        scratch_shapes=_att_scratch(t))
    return pl.pallas_call(
        _fox_kernel, grid_spec=grid_spec,
        out_shape=jax.ShapeDtypeStruct((s, w), jnp.bfloat16),
        compiler_params=_cparams(("arbitrary",)),
        name="fox_attention",
    )(qt, st, se, live, proj, proj, proj, proj, f_rows, f_rows)


def _outproj_kernel(x_ref, oa_ref, ob_ref, wa_ref, wb_ref, gate_ref, o_ref):
    y = jnp.dot(oa_ref[...], wa_ref[...], preferred_element_type=jnp.float32)
    y = y + jnp.dot(ob_ref[...], wb_ref[...], preferred_element_type=jnp.float32)
    o_ref[...] = x_ref[...] + gate_ref[...] * y


def _outproj(x, o_a, o_b, w_a, w_b, gate):
    s, d = x.shape
    tm, tn = ROW_TILE, 1024
    w = GROUP_WIDTH
    return pl.pallas_call(
        _outproj_kernel,
        grid=(s // tm, d // tn),
        in_specs=[pl.BlockSpec((tm, tn), lambda i, j: (i, j)),
                  pl.BlockSpec((tm, w), lambda i, j: (i, 0)),
                  pl.BlockSpec((tm, w), lambda i, j: (i, 0)),
                  pl.BlockSpec((w, tn), lambda i, j: (0, j)),
                  pl.BlockSpec((w, tn), lambda i, j: (0, j)),
                  pl.BlockSpec((1, tn), lambda i, j: (0, j))],
        out_specs=pl.BlockSpec((tm, tn), lambda i, j: (i, j)),
        out_shape=jax.ShapeDtypeStruct((s, d), jnp.float32),
        compiler_params=_cparams(("parallel", "arbitrary")),
        name="outproj",
    )(x, o_a, o_b, w_a, w_b, gate)


def _conv(y_ref, cw_ref, cb_ref, tm):
    h = CONV_HALO
    return (cw_ref[2:3, :] * y_ref[h:h + tm, :] + cw_ref[1:2, :] * y_ref[h - 1:h - 1 + tm, :]
            + cw_ref[0:1, :] * y_ref[h - 2:h - 2 + tm, :] + cb_ref[...])


def _ffn_kernel(x_ref, xh_ref, g_ref, sh_ref, sc_ref, wg_ref, wv_ref, cwg_ref, cwv_ref, cbg_ref, cbv_ref,
                wd_ref, gate_ref, gf_ref, o_ref, h_ref, yg_ref, yv_ref, acc_ref):
    i, f = pl.program_id(0), pl.program_id(1)
    tm = x_ref.shape[0]

    @pl.when(f == 0)
    def _():
        halo = _norm_mod(xh_ref[...], g_ref[...], sh_ref[...], sc_ref[...])
        h_ref[0:CONV_HALO, :] = jnp.where(i > 0, halo, 0.0).astype(jnp.bfloat16)
        h_ref[CONV_HALO:, :] = _norm_mod(x_ref[...], g_ref[...], sh_ref[...], sc_ref[...]).astype(jnp.bfloat16)
        acc_ref[...] = jnp.zeros(acc_ref.shape, jnp.float32)

    hb = h_ref[...]
    yg_ref[...] = jnp.dot(hb, wg_ref[...], preferred_element_type=jnp.float32)
    yv_ref[...] = jnp.dot(hb, wv_ref[...], preferred_element_type=jnp.float32)
    ug = _conv(yg_ref, cwg_ref, cbg_ref, tm)
    uv = _conv(yv_ref, cwv_ref, cbv_ref, tm)
    a = (ug / (1.0 + jnp.exp(-ug))) * uv
    acc_ref[...] += jnp.dot(a.astype(jnp.bfloat16), wd_ref[...], preferred_element_type=jnp.float32)

    @pl.when(f == pl.num_programs(1) - 1)
    def _():
        x2 = x_ref[...] + gate_ref[...] * acc_ref[...]
        y = x2 * lax.rsqrt(jnp.mean(x2 * x2, axis=-1, keepdims=True) + EPS)
        o_ref[...] = y * gf_ref[...]


def _ffn(x, g, shift, scale, w_up, conv_w, conv_b, w_down, gate, g_final):
    s, d = x.shape
    fp = w_down.shape[0]
    tm, tf = ROW_TILE, FF_TILE
    nf = fp // tf
    hb = tm // CONV_HALO
    row = lambda i, f: (0, 0)
    return pl.pallas_call(
        _ffn_kernel,
        grid=(s // tm, nf),
        in_specs=[pl.BlockSpec((tm, d), lambda i, f: (i, 0)),
                  pl.BlockSpec((CONV_HALO, d), lambda i, f: (jnp.maximum(i * hb - 1, 0), 0)),
                  pl.BlockSpec((1, d), row), pl.BlockSpec((1, d), row), pl.BlockSpec((1, d), row),
                  pl.BlockSpec((d, tf), lambda i, f: (0, f)),
                  pl.BlockSpec((d, tf), lambda i, f: (0, f + nf)),
                  pl.BlockSpec((CONV_WIDTH, tf), lambda i, f: (0, f)),
                  pl.BlockSpec((CONV_WIDTH, tf), lambda i, f: (0, f + nf)),
                  pl.BlockSpec((1, tf), lambda i, f: (0, f)),
                  pl.BlockSpec((1, tf), lambda i, f: (0, f + nf)),
                  pl.BlockSpec((tf, d), lambda i, f: (f, 0)),
                  pl.BlockSpec((1, d), row), pl.BlockSpec((1, d), row)],
        out_specs=pl.BlockSpec((tm, d), lambda i, f: (i, 0)),
        out_shape=jax.ShapeDtypeStruct((s, d), jnp.float32),
        scratch_shapes=[pltpu.VMEM((tm + CONV_HALO, d), jnp.bfloat16),
                        pltpu.VMEM((tm + CONV_HALO, tf), jnp.float32),
                        pltpu.VMEM((tm + CONV_HALO, tf), jnp.float32),
                        pltpu.VMEM((tm, d), jnp.float32)],
        compiler_params=_cparams(("parallel", "arbitrary")),
        name="conv_ffn",
    )(x, x, g, shift, scale, w_up, w_up, conv_w, conv_w, conv_b, conv_b, w_down, gate, g_final)


def _t5_bias_tiles(rel_bias, t):
    max_exact = N_BUCKETS // 2
    d = jnp.arange(2 * t, dtype=jnp.int32)
    df = jnp.maximum(d, 1).astype(jnp.float32)
    large = max_exact + (jnp.log(df / max_exact) / math.log(MAX_DISTANCE / max_exact)
                         * (N_BUCKETS - max_exact)).astype(jnp.int32)
    bucket = jnp.where(d < max_exact, d, jnp.minimum(large, N_BUCKETS - 1))
    rb = rel_bias.astype(jnp.float32)
    table = (rb[bucket] - rb[N_BUCKETS - 1]) * LOG2E
    period = 2 * t
    k = np.arange(period)
    tiles = []
    for off in (0, t):
        idx = np.clip(np.where(k < t, off - k, off + period - k), 0, 2 * t - 1)
        seq = jnp.tile(table[idx].T, (1, t))
        tiles.append(seq[:, :t * (period - 1)].reshape(N_HEADS, t, period - 1)[:, :, :t])
    return jnp.stack(tiles + [jnp.zeros_like(tiles[0])], axis=1)


def _pad_cols(a, n):
    return jnp.pad(a, ((0, 0), (0, n - a.shape[1])))


def kernel(x, c, rel_bias, w_ada, b_ada, g_attn, w_in, b_forget, w_out, g_mlp, w_up, conv_w, conv_b,
           w_down, g_final):
    b, s, d = x.shape
    assert b == 1 and s % ROW_TILE == 0 and s % DSA_Q_TILE == 0 and s % SEL_TILE == 0
    assert BIAS_TILE >= MAX_DISTANCE
    depth = w_ada.shape[0]
    assert depth == 1, "the fused MLP kernel applies the final norm, so it must be the last layer"
    d_ff = w_down.shape[1]
    ff_pad = -(-d_ff // FF_TILE) * FF_TILE
    n_sel = min(TOPK_MAX, s // 4)
    gw, nq = GROUP_WIDTH, IDX_HEADS * IDX_DIM
    bf = jnp.bfloat16
    xs = x[0]
    c_col = c.reshape(d, 1)
    bias_tiles = _t5_bias_tiles(rel_bias, BIAS_TILE)

    for l in range(depth):
        mod = _ada(c_col, w_ada[l], b_ada[l][None, :])
        shift_a, scale_a, gate_a, shift_m, scale_m, gate_m = [mod[:, k * d:(k + 1) * d] for k in range(6)]

        wi = w_in[l]
        o = np.cumsum([0, gw, gw, gw, nq, IDX_DIM, IDX_HEADS, gw, gw, gw, gw, N_HEADS])
        seg = lambda k: wi[:, o[k]:o[k + 1]]
        q_scale = HEAD_DIM ** -0.5 * LOG2E
        w_main = jnp.concatenate([seg(0) * q_scale, seg(1), seg(2),
                                  seg(6) * q_scale, seg(7), seg(8), seg(9)], axis=1).astype(bf)
        w_small = _pad_cols(jnp.concatenate([seg(3) * IDX_DIM ** -0.5, seg(4), seg(5), seg(10)], axis=1),
                            SMALL_WIDTH).astype(bf)

        proj, qi, small = _inproj(xs, g_attn[l][None, :], shift_a, scale_a, w_main, w_small)

        k_idx = small[:, :IDX_DIM].astype(bf)
        w_t = small[:, IDX_DIM:IDX_DIM + IDX_HEADS].T
        fb_t = small[:, IDX_DIM + IDX_HEADS:IDX_DIM + IDX_HEADS + N_HEADS].T
        f_rows = _fcum(fb_t.reshape(N_HEADS, s // LANES, LANES), b_forget[l]).reshape(N_HEADS, s)

        mask = _select(k_idx, qi.T, w_t, n_sel)
        o_a = _dsa(proj, mask, bias_tiles)
        o_b = _fox(proj, f_rows)

        wo = w_out[l].astype(bf)
        x1 = _outproj(xs, o_a, o_b, wo[:gw], wo[gw:], gate_a)

        wu = w_up[l]
        w_up_p = jnp.concatenate([_pad_cols(wu[:, :d_ff], ff_pad), _pad_cols(wu[:, d_ff:], ff_pad)], axis=1).astype(bf)
        cw = conv_w[l]
        cw_p = jnp.concatenate([_pad_cols(cw[:, :d_ff], ff_pad), _pad_cols(cw[:, d_ff:], ff_pad)], axis=1)
        cb = conv_b[l][None, :]
        cb_p = jnp.concatenate([_pad_cols(cb[:, :d_ff], ff_pad), _pad_cols(cb[:, d_ff:], ff_pad)], axis=1)
        w_down_p = jnp.pad(w_down[l], ((0, ff_pad - d_ff), (0, 0))).astype(bf)

        xs = _ffn(x1, g_mlp[l][None, :], shift_m, scale_m, w_up_p, cw_p, cb_p, w_down_p, gate_m,
                  g_final[None, :])

    return xs[None]
```

```python
import functools
import math

import jax
import jax.numpy as jnp
import numpy as np
from jax import lax
from jax.experimental import pallas as pl
from jax.experimental.pallas import tpu as pltpu

HEAD_DIM = 128
N_HEADS = 8
GROUP_WIDTH = N_HEADS * HEAD_DIM
IDX_HEADS = 8
IDX_DIM = 64
TOPK_MAX = 256
N_BUCKETS = 32
MAX_DISTANCE = 128
CONV_WIDTH = 3
EPS = 1e-6
NEG = -1e30
NEG_MASK = -(2.0 ** 100)

LANES = 128
SUBLANES = 8
VMEM_LIMIT = 56 * 1024 * 1024

ROW_TILE = 512
ATT_TILE = 512
DSA_Q_TILE = 1024
SEL_TILE = 256
BIAS_TILE = 256
LOG2E = math.log2(math.e)
SEL_ROWS = 128
SEL_CHUNK = 512
COUNT_ROWS = 64
SHORTLIST = 12
SKIP_LOG2 = 160.0
NORM_SLACK = 1.01
FOX_Q_COL, FOX_K_COL = 3, 4
RANK_TILE = 256
FF_TILE = 512
CONV_HALO = 16
SMALL_WIDTH = 640


def _cparams(sem):
    return pltpu.CompilerParams(dimension_semantics=sem, vmem_limit_bytes=VMEM_LIMIT)


def _ada_kernel(c_ref, w_ref, b_ref, o_ref):
    d = w_ref.shape[0]
    rows = 256

    def body(r, acc):
        sl = pl.ds(pl.multiple_of(r * rows, rows), rows)
        cc = c_ref[sl, :]
        ca = cc / (1.0 + jnp.exp(-cc))
        prod = w_ref[sl, :] * ca
        return acc + prod.reshape(rows // SUBLANES, SUBLANES, -1).sum(axis=0)

    acc = lax.fori_loop(0, d // rows, body, jnp.zeros((SUBLANES, w_ref.shape[1]), jnp.float32))
    o_ref[...] = jnp.sum(acc, axis=0, keepdims=True) + b_ref[...]


def _ada(c_col, w, b_row):
    d, n = w.shape
    tn = 1024
    return pl.pallas_call(
        _ada_kernel,
        grid=(n // tn,),
        in_specs=[pl.BlockSpec((d, 1), lambda j: (0, 0)),
                  pl.BlockSpec((d, tn), lambda j: (0, j)),
                  pl.BlockSpec((1, tn), lambda j: (0, j))],
        out_specs=pl.BlockSpec((1, tn), lambda j: (0, j)),
        out_shape=jax.ShapeDtypeStruct((1, n), jnp.float32),
        compiler_params=_cparams(("arbitrary",)),
        name="adaln",
    )(c_col, w, b_row)


def _norm_mod(x, g, shift, scale):
    y = x * lax.rsqrt(jnp.mean(x * x, axis=-1, keepdims=True) + EPS)
    return (y * g) * (1.0 + scale) + shift


def _inproj_kernel(x_ref, g_ref, sh_ref, sc_ref, wm_ref, ws_ref, om_ref, oq_ref, os_ref, on_ref, h_ref):
    j = pl.program_id(1)

    @pl.when(j == 0)
    def _():
        h = _norm_mod(x_ref[...], g_ref[...], sh_ref[...], sc_ref[...]).astype(jnp.bfloat16)
        h_ref[...] = h
        small = jnp.dot(h, ws_ref[...], preferred_element_type=jnp.float32)
        nq = IDX_HEADS * IDX_DIM
        oq_ref[...] = small[:, :nq].astype(jnp.bfloat16)
        os_ref[...] = small[:, nq:]
        on_ref[...] = jnp.zeros(on_ref.shape, jnp.float32)

    y = jnp.dot(h_ref[...], wm_ref[...], preferred_element_type=jnp.float32)
    om_ref[...] = y.astype(jnp.bfloat16)

    for col, lane0 in ((FOX_Q_COL, 0), (FOX_K_COL, N_HEADS)):
        @pl.when(j == col)
        def _():
            sq = y * y
            on_ref[:, lane0:lane0 + N_HEADS] = jnp.concatenate(
                [jnp.sum(sq[:, h * HEAD_DIM:(h + 1) * HEAD_DIM], axis=-1, keepdims=True)
                 for h in range(N_HEADS)], axis=1)


def _inproj(x, g, shift, scale, w_main, w_small):
    s, d = x.shape
    n_main = w_main.shape[1]
    tm, tn = ROW_TILE, GROUP_WIDTH
    nq = IDX_HEADS * IDX_DIM
    row = lambda i, j: (0, 0)
    return pl.pallas_call(
        _inproj_kernel,
        grid=(s // tm, n_main // tn),
        in_specs=[pl.BlockSpec((tm, d), lambda i, j: (i, 0)),
                  pl.BlockSpec((1, d), row), pl.BlockSpec((1, d), row), pl.BlockSpec((1, d), row),
                  pl.BlockSpec((d, tn), lambda i, j: (0, j)),
                  pl.BlockSpec((d, SMALL_WIDTH), row)],
        out_specs=[pl.BlockSpec((tm, tn), lambda i, j: (i, j)),
                   pl.BlockSpec((tm, nq), lambda i, j: (i, 0)),
                   pl.BlockSpec((tm, SMALL_WIDTH - nq), lambda i, j: (i, 0)),
                   pl.BlockSpec((tm, LANES), lambda i, j: (i, 0))],
        out_shape=[jax.ShapeDtypeStruct((s, n_main), jnp.bfloat16),
                   jax.ShapeDtypeStruct((s, nq), jnp.bfloat16),
                   jax.ShapeDtypeStruct((s, SMALL_WIDTH - nq), jnp.float32),
                   jax.ShapeDtypeStruct((s, LANES), jnp.float32)],
        scratch_shapes=[pltpu.VMEM((tm, d), jnp.bfloat16)],
        compiler_params=_cparams(("parallel", "arbitrary")),
        name="inproj",
    )(x, g, shift, scale, w_main, w_small)


def _fcum_kernel(fb_ref, b_ref, o_ref):
    r = fb_ref.shape[1]
    hi = lax.Precision.HIGHEST
    ci = lax.broadcasted_iota(jnp.int32, (LANES, LANES), 0)
    cj = lax.broadcasted_iota(jnp.int32, (LANES, LANES), 1)
    upper = (ci <= cj).astype(jnp.float32)
    ri = lax.broadcasted_iota(jnp.int32, (r, r), 0)
    rj = lax.broadcasted_iota(jnp.int32, (r, r), 1)
    strict_lower = (rj < ri).astype(jnp.float32)
    for h in range(N_HEADS):
        z = fb_ref[h] + b_ref[h]
        lf = jnp.minimum(z, 0.0) - jnp.log(1.0 + jnp.exp(-jnp.abs(z)))
        within = jnp.dot(lf, upper, precision=hi, preferred_element_type=jnp.float32)
        before = jnp.dot(strict_lower, within, precision=hi, preferred_element_type=jnp.float32)
        o_ref[h] = within + before[:, LANES - 1:LANES]


def _fcum(fb_t, b_forget):
    h, r, _ = fb_t.shape
    return pl.pallas_call(
        _fcum_kernel,
        in_specs=[pl.BlockSpec(memory_space=pltpu.VMEM),
                  pl.BlockSpec(memory_space=pltpu.SMEM)],
        out_specs=pl.BlockSpec(memory_space=pltpu.VMEM),
        out_shape=jax.ShapeDtypeStruct((h, r, LANES), jnp.float32),
        name="forget_cumsum",
    )(fb_t, b_forget)


def _ordered_bits(v):
    return v ^ ((v >> 31) & jnp.int32(0x7FFFFFFF))


def _float_to_key(x):
    return _ordered_bits(pltpu.bitcast(x, jnp.int32))


def _key_to_float(key):
    return pltpu.bitcast(_ordered_bits(key), jnp.float32)


def _fold_rows(x, op):
    parts = [x[r:r + SUBLANES] for r in range(0, x.shape[0], SUBLANES)]
    while len(parts) > 1:
        parts = [op(parts[k], parts[k + 1]) for k in range(0, len(parts) - 1, 2)] + (
            [parts[-1]] if len(parts) % 2 else [])
    return parts[0]


def _select_kernel(k_ref, qit_ref, wt_ref, mask_ref, s_ref, top_ref, short_ref, *, seq, n_sel):
    tq = qit_ref.shape[1]
    ck = SEL_CHUNK
    i = pl.program_id(0)
    n_chunks = ((i + 1) * tq + ck - 1) // ck
    n_rest = (seq - n_chunks * ck).astype(jnp.float32)
    q_pos = i * tq + lax.broadcasted_iota(jnp.int32, (SEL_ROWS, tq), 1)
    key_iota = lax.broadcasted_iota(jnp.int32, (SEL_ROWS, tq), 0)
    wts = wt_ref[...] * (IDX_HEADS ** -0.5)
    n_slots = top_ref.shape[0]

    top_ref[...] = jnp.full(top_ref.shape, -jnp.inf, jnp.float32)

    def score_chunk(c, signs):
        pos, nonneg = signs
        for r in range(0, ck, SEL_ROWS):
            r0 = pl.multiple_of(c * ck + r, SEL_ROWS)
            kc = k_ref[pl.ds(r0, SEL_ROWS), :]
            acc = jnp.zeros((SEL_ROWS, tq), jnp.float32)
            for h in range(IDX_HEADS):
                lg = jnp.dot(kc, qit_ref[h * IDX_DIM:(h + 1) * IDX_DIM, :],
                             preferred_element_type=jnp.float32)
                acc = acc + jnp.maximum(lg, 0.0) * wts[h:h + 1, :]
            sc = jnp.where(r0 + key_iota <= q_pos, acc, NEG)
            s_ref[pl.ds(r0, SEL_ROWS), :] = sc
            slot = r % n_slots
            top_ref[slot:slot + SEL_ROWS, :] = jnp.maximum(top_ref[slot:slot + SEL_ROWS, :], sc)
            pos = pos + _fold_rows(jnp.where(sc > 0.0, 1.0, 0.0), jnp.add)
            nonneg = nonneg + _fold_rows(jnp.where(sc >= 0.0, 1.0, 0.0), jnp.add)
        return pos, nonneg

    zero_count = jnp.zeros((SUBLANES, tq), jnp.float32)
    pos, nonneg = lax.fori_loop(0, n_chunks, score_chunk, (zero_count, zero_count))
    n_pos = jnp.sum(pos, axis=0, keepdims=True)
    n_nonneg = jnp.sum(nonneg, axis=0, keepdims=True)

    def count(pred, ref=s_ref, chunks=n_chunks):
        def body(c, accs):
            for r in range(0, ck, COUNT_ROWS):
                x = ref[pl.ds(pl.multiple_of(c * ck + r, COUNT_ROWS), COUNT_ROWS), :]
                accs = tuple(a + _fold_rows(jnp.where(f(x), 1.0, 0.0), jnp.add) for a, f in zip(accs, pred))
            return accs
        zero = jnp.zeros((SUBLANES, tq), jnp.float32)
        accs = lax.fori_loop(0, chunks, body, (zero,) * len(pred))
        return [jnp.sum(a, axis=0, keepdims=True) for a in accs]

    def count_ge(cf):
        return count([lambda x: x >= cf])[0] + jnp.where(NEG >= cf, n_rest, 0.0)

    top = top_ref[...]
    low_key = _float_to_key(jnp.min(_fold_rows(top, jnp.minimum), axis=0, keepdims=True))
    max_key = _float_to_key(jnp.max(_fold_rows(top, jnp.maximum), axis=0, keepdims=True))
    hi0 = jnp.maximum(max_key, _float_to_key(jnp.full((1, tq), NEG, jnp.float32))) + 1
    lo0 = jnp.full((1, tq), -2 ** 31, jnp.int32)
    max_passes = 96

    at_zero = jnp.logical_and(n_pos < n_sel, n_nonneg >= n_sel)
    above_zero = n_pos >= n_sel
    below_zero = n_nonneg < n_sel
    zero_key = jnp.zeros((1, tq), jnp.int32)
    lo0 = jnp.where(below_zero, lo0, zero_key)
    c_lo0 = jnp.where(below_zero, float(seq), n_nonneg)
    hi0 = jnp.where(above_zero, hi0, jnp.where(at_zero, zero_key + 2 ** 23, zero_key - 1))
    c_hi0 = jnp.where(above_zero, 0.0, jnp.where(at_zero, n_pos, n_nonneg))
    done0 = jnp.where(at_zero, 1, 0)

    def searching(st):
        p, _, _, _, _, done = st
        return jnp.logical_and(p < max_passes, jnp.min(done) == 0)

    def search_pass(count_fn, st):
        p, lo, hi, c_lo, c_hi, done = st
        mid_val = _float_to_key(0.5 * _key_to_float(lo) + 0.5 * _key_to_float(hi))
        mid_key = (lo >> 1) + (hi >> 1) + (lo & hi & 1)
        cand = jnp.where(p == 0, low_key, jnp.where((p & 3) == 0, mid_key, mid_val))
        cand = jnp.minimum(jnp.maximum(cand, lo + 1), hi - 1)
        cnt = count_fn(_key_to_float(cand))
        active = done == 0
        up = jnp.logical_and(active, cnt >= n_sel)
        down = jnp.logical_and(active, cnt < n_sel)
        lo, c_lo = jnp.where(up, cand, lo), jnp.where(up, cnt, c_lo)
        hi, c_hi = jnp.where(down, cand, hi), jnp.where(down, cnt, c_hi)
        finished = jnp.logical_and(active, jnp.logical_or(cnt == n_sel, hi - 1 <= lo))
        return p + 1, lo, hi, c_lo, c_hi, jnp.where(finished, 1, done)

    def narrowing(st):
        p, _, _, c_lo, c_hi, done = st
        wide = jnp.logical_and(done == 0, c_lo - c_hi > SHORTLIST)
        return jnp.logical_and(p < max_passes, jnp.max(jnp.where(wide, 1, 0)) > 0)

    st = lax.while_loop(narrowing, functools.partial(search_pass, count_ge),
                        (jnp.int32(0), lo0, hi0, c_lo0, c_hi0, done0))

    p1, lo1, hi1, c_lo1, c_hi1, done1 = st
    hi_f = _key_to_float(hi1)
    short_ref[...] = jnp.full(short_ref.shape, -jnp.inf, jnp.float32)

    def shortlist_chunk(c, _):
        for r in range(0, ck, COUNT_ROWS):
            x = s_ref[pl.ds(pl.multiple_of(c * ck + r, COUNT_ROWS), COUNT_ROWS), :]
            y = jnp.where(x < hi_f, x, -jnp.inf)
            first = short_ref[r:r + COUNT_ROWS, :]
            second = short_ref[ck + r:ck + r + COUNT_ROWS, :]
            short_ref[r:r + COUNT_ROWS, :] = jnp.maximum(first, y)
            short_ref[ck + r:ck + r + COUNT_ROWS, :] = jnp.maximum(second, jnp.minimum(first, y))
        return 0

    lax.fori_loop(0, n_chunks, shortlist_chunk, 0)
    lo_f = _key_to_float(lo1)
    listed = count([lambda x: x >= lo_f], short_ref, short_ref.shape[0] // ck)[0]
    complete = listed == c_lo1 - c_hi1

    def count_short(cf):
        return c_hi1 + count([lambda x: x >= cf], short_ref, short_ref.shape[0] // ck)[0]

    st = lax.while_loop(searching, functools.partial(search_pass, count_short),
                        (p1, lo1, hi1, c_lo1, c_hi1, jnp.where(complete, done1, 1)))
    p2, lo2, hi2, c_lo2, c_hi2, done2 = st
    pick = lambda a, b: jnp.where(complete, a, b)

    st = lax.while_loop(searching, functools.partial(search_pass, count_ge),
                        (p2, pick(lo2, lo1), pick(hi2, hi1), pick(c_lo2, c_lo1), pick(c_hi2, c_hi1),
                         pick(done2, done1)))
    _, lo, _, c_lo, c_hi, _ = st
    thr = _key_to_float(lo)

    need = jnp.where(c_lo > n_sel, n_sel - c_hi, float(seq))
    rt = RANK_TILE
    earlier = (lax.broadcasted_iota(jnp.int32, (rt, rt), 1)
               < lax.broadcasted_iota(jnp.int32, (rt, rt), 0)).astype(jnp.bfloat16)

    def mask_chunk(c, ties_before):
        for t0 in range(0, ck, rt):
            c0 = pl.multiple_of(c * ck + t0, rt)
            tie = jnp.where(s_ref[pl.ds(c0, rt), :] == thr, 1.0, 0.0)
            rank = ties_before + jnp.dot(earlier, tie.astype(jnp.bfloat16),
                                         preferred_element_type=jnp.float32)
            for r in range(0, rt, SEL_ROWS):
                r0 = pl.multiple_of(c0 + r, SEL_ROWS)
                x = s_ref[pl.ds(r0, SEL_ROWS), :]
                kept_tie = jnp.logical_and(x == thr, rank[r:r + SEL_ROWS] < need)
                keep = jnp.logical_and(jnp.logical_or(x > thr, kept_tie), r0 + key_iota <= q_pos)
                mask_ref[:, pl.ds(r0, SEL_ROWS)] = jnp.where(keep, 0.0, NEG_MASK).T.astype(jnp.bfloat16)
            ties_before = ties_before + jnp.sum(_fold_rows(tie, jnp.add), axis=0, keepdims=True)
        return ties_before

    lax.fori_loop(0, n_chunks, mask_chunk, jnp.zeros((1, tq), jnp.float32))

    def fill_chunk(c, _):
        r0 = pl.multiple_of(c * ck, ck)
        mask_ref[:, pl.ds(r0, ck)] = jnp.full((tq, ck), NEG_MASK, jnp.bfloat16)
        return 0

    lax.fori_loop(n_chunks, seq // ck, fill_chunk, 0)


def _select(k_idx, qi_t, w_t, n_sel):
    s = k_idx.shape[0]
    tq = SEL_TILE
    assert n_sel <= RANK_TILE and s % SEL_CHUNK == 0 and SEL_CHUNK % RANK_TILE == 0
    return pl.pallas_call(
        functools.partial(_select_kernel, seq=s, n_sel=n_sel),
        grid=(s // tq,),
        in_specs=[pl.BlockSpec(k_idx.shape, lambda i: (0, 0)),
                  pl.BlockSpec((qi_t.shape[0], tq), lambda i: (0, i)),
                  pl.BlockSpec((IDX_HEADS, tq), lambda i: (0, i))],
        out_specs=pl.BlockSpec((tq, s), lambda i: (i, 0)),
        out_shape=jax.ShapeDtypeStruct((s, s), jnp.bfloat16),
        scratch_shapes=[pltpu.VMEM((s, tq), jnp.float32), pltpu.VMEM((RANK_TILE, tq), jnp.float32),
                        pltpu.VMEM((2 * SEL_CHUNK, tq), jnp.float32)],
        compiler_params=_cparams(("parallel",)),
        name="dsa_select",
    )(k_idx, qi_t, w_t)


def _qk(a, b):
    return lax.dot_general(a, b, (((1,), (1,)), ((), ())), preferred_element_type=jnp.float32)


def _pair_tables(nq):
    qt = np.concatenate([np.full(q + 1, q, np.int32) for q in range(nq)])
    st = np.concatenate([np.arange(q + 1, dtype=np.int32) for q in range(nq)])
    return jnp.asarray(qt), jnp.asarray(st)


def _softmax_step(h, s, v, m_ref, acc_ref):
    ts = s.shape[1]
    m_prev = m_ref[h]
    m_new = jnp.maximum(m_prev, jnp.max(s, axis=-1, keepdims=True))
    alpha = jnp.exp2(m_prev - m_new)
    pm = jnp.exp2(s - jnp.tile(m_new, (1, ts // LANES))).astype(jnp.bfloat16)
    v1 = jnp.concatenate([v, jnp.ones((ts, HEAD_DIM), jnp.bfloat16)], axis=1)
    acc_ref[h] = (jnp.tile(alpha, (1, 2 * HEAD_DIM // LANES)) * acc_ref[h]
                  + jnp.dot(pm, v1, preferred_element_type=jnp.float32))
    m_ref[h] = m_new


def _init_softmax(m_ref, acc_ref):
    m_ref[...] = jnp.full(m_ref.shape, NEG_MASK, jnp.float32)
    acc_ref[...] = jnp.zeros(acc_ref.shape, jnp.float32)


def _softmax_result(h, acc_ref):
    return acc_ref[h, :, :HEAD_DIM] / acc_ref[h, :, HEAD_DIM:]


def _att_scratch(tq):
    return [pltpu.VMEM((N_HEADS, tq, LANES), jnp.float32),
            pltpu.VMEM((N_HEADS, tq, 2 * HEAD_DIM), jnp.float32)]


def _dsa_kernel(qt_ref, st_ref, q_ref, k_ref, v_ref, mask_ref, bias_ref, o_ref, m_ref, acc_ref):
    p = pl.program_id(0)
    qi, si = qt_ref[p], st_ref[p]
    tq, ts = mask_ref.shape
    bt = bias_ref.shape[-1]
    lag00 = (qi * tq - si * ts) // bt
    lags = [[lag00 + a - b for b in range(ts // bt)] for a in range(tq // bt)]
    quad = [[jnp.where(lag == 0, 0, jnp.where(lag == 1, 1, 2)) for lag in row] for row in lags]
    near = lags[0][-1] <= 1
    last = (si + 1) * ts >= (qi + 1) * tq

    @pl.when(si == 0)
    def _():
        _init_softmax(m_ref, acc_ref)

    def sweep(with_bias):
        maskf = mask_ref[...].astype(jnp.float32)
        for h in range(N_HEADS):
            hs = slice(h * HEAD_DIM, (h + 1) * HEAD_DIM)
            s = _qk(q_ref[:, hs], k_ref[:, hs]) + maskf
            if with_bias:
                s = s + jnp.concatenate(
                    [jnp.concatenate([bias_ref[h, idx] for idx in row], axis=1) for row in quad], axis=0)
            _softmax_step(h, s, v_ref[:, hs], m_ref, acc_ref)

    pl.when(near)(functools.partial(sweep, True))
    pl.when(jnp.logical_not(near))(functools.partial(sweep, False))

    @pl.when(last)
    def _():
        for h in range(N_HEADS):
            hs = slice(h * HEAD_DIM, (h + 1) * HEAD_DIM)
            o_ref[:, hs] = _softmax_result(h, acc_ref).astype(o_ref.dtype)


def _dsa(proj, mask, bias_tiles):
    s = proj.shape[0]
    tq, ts = DSA_Q_TILE, ATT_TILE
    assert tq % ts == 0 and ts % BIAS_TILE == 0 and bias_tiles.shape[1:] == (3, BIAS_TILE, BIAS_TILE)
    r = tq // ts
    nq = s // tq
    qt = jnp.asarray(np.concatenate([np.full(r * (q + 1), q, np.int32) for q in range(nq)]))
    st = jnp.asarray(np.concatenate([np.arange(r * (q + 1), dtype=np.int32) for q in range(nq)]))
    w = GROUP_WIDTH
    grid_spec = pltpu.PrefetchScalarGridSpec(
        num_scalar_prefetch=2,
        grid=(qt.shape[0],),
        in_specs=[pl.BlockSpec((tq, w), lambda p, qt, st: (qt[p], 0)),
                  pl.BlockSpec((ts, w), lambda p, qt, st: (st[p], 1)),
                  pl.BlockSpec((ts, w), lambda p, qt, st: (st[p], 2)),
                  pl.BlockSpec((tq, ts), lambda p, qt, st: (qt[p], st[p])),
                  pl.BlockSpec(bias_tiles.shape, lambda p, qt, st: (0, 0, 0, 0))],
        out_specs=pl.BlockSpec((tq, w), lambda p, qt, st: (qt[p], 0)),
        scratch_shapes=_att_scratch(tq))
    return pl.pallas_call(
        _dsa_kernel, grid_spec=grid_spec,
        out_shape=jax.ShapeDtypeStruct((s, w), jnp.bfloat16),
        compiler_params=_cparams(("arbitrary",)),
        name="dsa_attention",
    )(qt, st, proj, proj, proj, mask, bias_tiles)


def _fox_kernel(qt_ref, st_ref, se_ref, live_ref, q_ref, k_ref, v_ref, g_ref, fq_ref, fs_ref, o_ref,
                m_ref, acc_ref):
    p = pl.program_id(0)
    qi, si = qt_ref[p], st_ref[p]
    tq, ts = q_ref.shape[0], k_ref.shape[0]

    @pl.when(si == 0)
    def _():
        _init_softmax(m_ref, acc_ref)

    def sweep(diagonal):
        if diagonal:
            causal = (lax.broadcasted_iota(jnp.int32, (tq, ts), 1)
                      <= lax.broadcasted_iota(jnp.int32, (tq, ts), 0))
        for h in range(N_HEADS):
            hs = slice(h * HEAD_DIM, (h + 1) * HEAD_DIM)
            decay = (fq_ref[h:h + 1, 0:1] - fs_ref[h:h + 1, :]) * LOG2E
            s = _qk(q_ref[:, hs], k_ref[:, hs]) + decay
            if diagonal:
                s = jnp.where(causal, s, NEG_MASK)
            _softmax_step(h, s, v_ref[:, hs], m_ref, acc_ref)

    pl.when(si == qi)(functools.partial(sweep, True))
    pl.when(jnp.logical_and(si != qi, live_ref[p] != 0))(functools.partial(sweep, False))

    @pl.when(si == qi)
    def _():
        for h in range(N_HEADS):
            hs = slice(h * HEAD_DIM, (h + 1) * HEAD_DIM)
            g = g_ref[:, hs].astype(jnp.float32)
            o_ref[:, hs] = (_softmax_result(h, acc_ref) / (1.0 + jnp.exp(-g))).astype(o_ref.dtype)


def _fox_schedule(sq_norms, f_rows, qt, st, t):
    n = sq_norms.shape[0] // t
    tile_norm = jnp.sqrt(jnp.max(sq_norms[:, :2 * N_HEADS].reshape(n, t, 2 * N_HEADS), axis=1)).T
    qn, kn = tile_norm[:N_HEADS], tile_norm[N_HEADS:]
    f = f_rows.reshape(N_HEADS, n, t)
    f_min, f_max = jnp.min(f, axis=-1), jnp.max(f, axis=-1)
    bound = (NORM_SLACK * qn[:, :, None] * (kn[:, None, :] + kn[:, :, None])
             - (f_min[:, None, :] - f_max[:, :, None]) * LOG2E)
    live_tiles = jnp.any(bound > -SKIP_LOG2, axis=0)
    live = jnp.logical_or(live_tiles[qt, st], qt == st)
    steps = jnp.arange(qt.shape[0], dtype=jnp.int32)
    next_live = lax.cummin(jnp.where(live, steps, qt.shape[0]), axis=0, reverse=True)
    return st[next_live], live.astype(jnp.int32)


def _fox(proj, sq_norms, f_rows):
    s = proj.shape[0]
    t = ATT_TILE
    qt, st = _pair_tables(s // t)
    se, live = _fox_schedule(sq_norms, f_rows, qt, st, t)
    w = GROUP_WIDTH
    grid_spec = pltpu.PrefetchScalarGridSpec(
        num_scalar_prefetch=4,
        grid=(qt.shape[0],),
        in_specs=[pl.BlockSpec((t, w), lambda p, qt, st, se, lv: (qt[p], 3)),
                  pl.BlockSpec((t, w), lambda p, qt, st, se, lv: (se[p], 4)),
                  pl.BlockSpec((t, w), lambda p, qt, st, se, lv: (se[p], 5)),
                  pl.BlockSpec((t, w), lambda p, qt, st, se, lv: (qt[p], 6)),
                  pl.BlockSpec((N_HEADS, t), lambda p, qt, st, se, lv: (0, qt[p])),
                  pl.BlockSpec((N_HEADS, t), lambda p, qt, st, se, lv: (0, se[p]))],
        out_specs=pl.BlockSpec((t, w), lambda p, qt, st, se, lv: (qt[p], 0)),
        scratch_shapes=_att_scratch(t))
    return pl.pallas_call(
        _fox_kernel, grid_spec=grid_spec,
        out_shape=jax.ShapeDtypeStruct((s, w), jnp.bfloat16),
        compiler_params=_cparams(("arbitrary",)),
        name="fox_attention",
    )(qt, st, se, live, proj, proj, proj, proj, f_rows, f_rows)


def _outproj_kernel(x_ref, oa_ref, ob_ref, wa_ref, wb_ref, gate_ref, o_ref):
    y = jnp.dot(oa_ref[...], wa_ref[...], preferred_element_type=jnp.float32)
    y = y + jnp.dot(ob_ref[...], wb_ref[...], preferred_element_type=jnp.float32)
    o_ref[...] = x_ref[...] + gate_ref[...] * y


def _outproj(x, o_a, o_b, w_a, w_b, gate):
    s, d = x.shape
    tm, tn = ROW_TILE, 1024
    w = GROUP_WIDTH
    return pl.pallas_call(
        _outproj_kernel,
        grid=(s // tm, d // tn),
        in_specs=[pl.BlockSpec((tm, tn), lambda i, j: (i, j)),
                  pl.BlockSpec((tm, w), lambda i, j: (i, 0)),
                  pl.BlockSpec((tm, w), lambda i, j: (i, 0)),
                  pl.BlockSpec((w, tn), lambda i, j: (0, j)),
                  pl.BlockSpec((w, tn), lambda i, j: (0, j)),
                  pl.BlockSpec((1, tn), lambda i, j: (0, j))],
        out_specs=pl.BlockSpec((tm, tn), lambda i, j: (i, j)),
        out_shape=jax.ShapeDtypeStruct((s, d), jnp.float32),
        compiler_params=_cparams(("parallel", "arbitrary")),
        name="outproj",
    )(x, o_a, o_b, w_a, w_b, gate)


def _conv(y_ref, cw_ref, cb_ref, tm):
    h = CONV_HALO
    return (cw_ref[2:3, :] * y_ref[h:h + tm, :] + cw_ref[1:2, :] * y_ref[h - 1:h - 1 + tm, :]
            + cw_ref[0:1, :] * y_ref[h - 2:h - 2 + tm, :] + cb_ref[...])


def _ffn_kernel(x_ref, xh_ref, g_ref, sh_ref, sc_ref, wg_ref, wv_ref, cwg_ref, cwv_ref, cbg_ref, cbv_ref,
                wd_ref, gate_ref, gf_ref, o_ref, h_ref, yg_ref, yv_ref, acc_ref):
    i, f = pl.program_id(0), pl.program_id(1)
    tm = x_ref.shape[0]

    @pl.when(f == 0)
    def _():
        halo = _norm_mod(xh_ref[...], g_ref[...], sh_ref[...], sc_ref[...])
        h_ref[0:CONV_HALO, :] = jnp.where(i > 0, halo, 0.0).astype(jnp.bfloat16)
        h_ref[CONV_HALO:, :] = _norm_mod(x_ref[...], g_ref[...], sh_ref[...], sc_ref[...]).astype(jnp.bfloat16)
        acc_ref[...] = jnp.zeros(acc_ref.shape, jnp.float32)

    hb = h_ref[...]
    yg_ref[...] = jnp.dot(hb, wg_ref[...], preferred_element_type=jnp.float32)
    yv_ref[...] = jnp.dot(hb, wv_ref[...], preferred_element_type=jnp.float32)
    ug = _conv(yg_ref, cwg_ref, cbg_ref, tm)
    uv = _conv(yv_ref, cwv_ref, cbv_ref, tm)
    a = (ug / (1.0 + jnp.exp(-ug))) * uv
    acc_ref[...] += jnp.dot(a.astype(jnp.bfloat16), wd_ref[...], preferred_element_type=jnp.float32)

    @pl.when(f == pl.num_programs(1) - 1)
    def _():
        x2 = x_ref[...] + gate_ref[...] * acc_ref[...]
        y = x2 * lax.rsqrt(jnp.mean(x2 * x2, axis=-1, keepdims=True) + EPS)
        o_ref[...] = y * gf_ref[...]


def _ffn(x, g, shift, scale, w_up, conv_w, conv_b, w_down, gate, g_final):
    s, d = x.shape
    fp = w_down.shape[0]
    tm, tf = ROW_TILE, FF_TILE
    nf = fp // tf
    hb = tm // CONV_HALO
    row = lambda i, f: (0, 0)
    return pl.pallas_call(
        _ffn_kernel,
        grid=(s // tm, nf),
        in_specs=[pl.BlockSpec((tm, d), lambda i, f: (i, 0)),
                  pl.BlockSpec((CONV_HALO, d), lambda i, f: (jnp.maximum(i * hb - 1, 0), 0)),
                  pl.BlockSpec((1, d), row), pl.BlockSpec((1, d), row), pl.BlockSpec((1, d), row),
                  pl.BlockSpec((d, tf), lambda i, f: (0, f)),
                  pl.BlockSpec((d, tf), lambda i, f: (0, f + nf)),
                  pl.BlockSpec((CONV_WIDTH, tf), lambda i, f: (0, f)),
                  pl.BlockSpec((CONV_WIDTH, tf), lambda i, f: (0, f + nf)),
                  pl.BlockSpec((1, tf), lambda i, f: (0, f)),
                  pl.BlockSpec((1, tf), lambda i, f: (0, f + nf)),
                  pl.BlockSpec((tf, d), lambda i, f: (f, 0)),
                  pl.BlockSpec((1, d), row), pl.BlockSpec((1, d), row)],
        out_specs=pl.BlockSpec((tm, d), lambda i, f: (i, 0)),
        out_shape=jax.ShapeDtypeStruct((s, d), jnp.float32),
        scratch_shapes=[pltpu.VMEM((tm + CONV_HALO, d), jnp.bfloat16),
                        pltpu.VMEM((tm + CONV_HALO, tf), jnp.float32),
                        pltpu.VMEM((tm + CONV_HALO, tf), jnp.float32),
                        pltpu.VMEM((tm, d), jnp.float32)],
        compiler_params=_cparams(("parallel", "arbitrary")),
        name="conv_ffn",
    )(x, x, g, shift, scale, w_up, w_up, conv_w, conv_w, conv_b, conv_b, w_down, gate, g_final)


def _t5_bias_tiles(rel_bias, t):
    max_exact = N_BUCKETS // 2
    d = jnp.arange(2 * t, dtype=jnp.int32)
    df = jnp.maximum(d, 1).astype(jnp.float32)
    large = max_exact + (jnp.log(df / max_exact) / math.log(MAX_DISTANCE / max_exact)
                         * (N_BUCKETS - max_exact)).astype(jnp.int32)
    bucket = jnp.where(d < max_exact, d, jnp.minimum(large, N_BUCKETS - 1))
    rb = rel_bias.astype(jnp.float32)
    table = (rb[bucket] - rb[N_BUCKETS - 1]) * LOG2E
    period = 2 * t
    k = np.arange(period)
    tiles = []
    for off in (0, t):
        idx = np.clip(np.where(k < t, off - k, off + period - k), 0, 2 * t - 1)
        seq = jnp.tile(table[idx].T, (1, t))
        tiles.append(seq[:, :t * (period - 1)].reshape(N_HEADS, t, period - 1)[:, :, :t])
    return jnp.stack(tiles + [jnp.zeros_like(tiles[0])], axis=1)


def _pad_cols(a, n):
    return jnp.pad(a, ((0, 0), (0, n - a.shape[1])))


def kernel(x, c, rel_bias, w_ada, b_ada, g_attn, w_in, b_forget, w_out, g_mlp, w_up, conv_w, conv_b,
           w_down, g_final):
    b, s, d = x.shape
    assert b == 1 and s % ROW_TILE == 0 and s % DSA_Q_TILE == 0 and s % SEL_TILE == 0
    assert BIAS_TILE >= MAX_DISTANCE
    depth = w_ada.shape[0]
    assert depth == 1, "the fused MLP kernel applies the final norm, so it must be the last layer"
    d_ff = w_down.shape[1]
    ff_pad = -(-d_ff // FF_TILE) * FF_TILE
    n_sel = min(TOPK_MAX, s // 4)
    gw, nq = GROUP_WIDTH, IDX_HEADS * IDX_DIM
    bf = jnp.bfloat16
    xs = x[0]
    c_col = c.reshape(d, 1)
    bias_tiles = _t5_bias_tiles(rel_bias, BIAS_TILE)

    for l in range(depth):
        mod = _ada(c_col, w_ada[l], b_ada[l][None, :])
        shift_a, scale_a, gate_a, shift_m, scale_m, gate_m = [mod[:, k * d:(k + 1) * d] for k in range(6)]

        wi = w_in[l]
        o = np.cumsum([0, gw, gw, gw, nq, IDX_DIM, IDX_HEADS, gw, gw, gw, gw, N_HEADS])
        seg = lambda k: wi[:, o[k]:o[k + 1]]
        q_scale = HEAD_DIM ** -0.5 * LOG2E
        w_main = jnp.concatenate([seg(0) * q_scale, seg(1), seg(2),
                                  seg(6) * q_scale, seg(7), seg(8), seg(9)], axis=1).astype(bf)
        w_small = _pad_cols(jnp.concatenate([seg(3) * IDX_DIM ** -0.5, seg(4), seg(5), seg(10)], axis=1),
                            SMALL_WIDTH).astype(bf)

        proj, qi, small, sq_norms = _inproj(xs, g_attn[l][None, :], shift_a, scale_a, w_main, w_small)

        k_idx = small[:, :IDX_DIM].astype(bf)
        w_t = small[:, IDX_DIM:IDX_DIM + IDX_HEADS].T
        fb_t = small[:, IDX_DIM + IDX_HEADS:IDX_DIM + IDX_HEADS + N_HEADS].T
        f_rows = _fcum(fb_t.reshape(N_HEADS, s // LANES, LANES), b_forget[l]).reshape(N_HEADS, s)

        mask = _select(k_idx, qi.T, w_t, n_sel)
        o_a = _dsa(proj, mask, bias_tiles)
        o_b = _fox(proj, sq_norms, f_rows)

        wo = w_out[l].astype(bf)
        x1 = _outproj(xs, o_a, o_b, wo[:gw], wo[gw:], gate_a)

        wu = w_up[l]
        w_up_p = jnp.concatenate([_pad_cols(wu[:, :d_ff], ff_pad), _pad_cols(wu[:, d_ff:], ff_pad)], axis=1).astype(bf)
        cw = conv_w[l]
        cw_p = jnp.concatenate([_pad_cols(cw[:, :d_ff], ff_pad), _pad_cols(cw[:, d_ff:], ff_pad)], axis=1)
        cb = conv_b[l][None, :]
        cb_p = jnp.concatenate([_pad_cols(cb[:, :d_ff], ff_pad), _pad_cols(cb[:, d_ff:], ff_pad)], axis=1)
        w_down_p = jnp.pad(w_down[l], ((0, ff_pad - d_ff), (0, 0))).astype(bf)

        xs = _ffn(x1, g_mlp[l][None, :], shift_m, scale_m, w_up_p, cw_p, cb_p, w_down_p, gate_m,
                  g_final[None, :])

    return xs[None]
```

```python
import functools
import math

import jax
import jax.numpy as jnp
import numpy as np
from jax import lax
from jax.experimental import pallas as pl
from jax.experimental.pallas import tpu as pltpu

HEAD_DIM = 128
N_HEADS = 8
GROUP_WIDTH = N_HEADS * HEAD_DIM
IDX_HEADS = 8
IDX_DIM = 64
TOPK_MAX = 256
N_BUCKETS = 32
MAX_DISTANCE = 128
CONV_WIDTH = 3
EPS = 1e-6
NEG = -1e30
NEG_MASK = -(2.0 ** 100)

LANES = 128
SUBLANES = 8
VMEM_LIMIT = 56 * 1024 * 1024

ROW_TILE = 512
ATT_TILE = 512
DSA_Q_TILE = 1024
SEL_TILE = 256
BIAS_TILE = 256
LOG2E = math.log2(math.e)
SEL_ROWS = 128
SEL_CHUNK = 512
COUNT_ROWS = 64
SHORTLIST = 12
SKIP_LOG2 = 160.0
NORM_SLACK = 1.01
FOX_Q_COL, FOX_K_COL = 3, 4
RANK_TILE = 256
FF_TILE = 512
CONV_HALO = 16
SMALL_WIDTH = 640


def _cparams(sem):
    return pltpu.CompilerParams(dimension_semantics=sem, vmem_limit_bytes=VMEM_LIMIT)


def _ada_kernel(c_ref, w_ref, b_ref, o_ref):
    d = w_ref.shape[0]
    rows = 256

    def body(r, acc):
        sl = pl.ds(pl.multiple_of(r * rows, rows), rows)
        cc = c_ref[sl, :]
        ca = cc / (1.0 + jnp.exp(-cc))
        prod = w_ref[sl, :] * ca
        return acc + prod.reshape(rows // SUBLANES, SUBLANES, -1).sum(axis=0)

    acc = lax.fori_loop(0, d // rows, body, jnp.zeros((SUBLANES, w_ref.shape[1]), jnp.float32))
    o_ref[...] = jnp.sum(acc, axis=0, keepdims=True) + b_ref[...]


def _ada(c_col, w, b_row):
    d, n = w.shape
    tn = 1024
    return pl.pallas_call(
        _ada_kernel,
        grid=(n // tn,),
        in_specs=[pl.BlockSpec((d, 1), lambda j: (0, 0)),
                  pl.BlockSpec((d, tn), lambda j: (0, j)),
                  pl.BlockSpec((1, tn), lambda j: (0, j))],
        out_specs=pl.BlockSpec((1, tn), lambda j: (0, j)),
        out_shape=jax.ShapeDtypeStruct((1, n), jnp.float32),
        compiler_params=_cparams(("arbitrary",)),
        name="adaln",
    )(c_col, w, b_row)


def _norm_mod(x, g, shift, scale):
    y = x * lax.rsqrt(jnp.mean(x * x, axis=-1, keepdims=True) + EPS)
    return (y * g) * (1.0 + scale) + shift


def _inproj_kernel(x_ref, g_ref, sh_ref, sc_ref, wm_ref, ws_ref, om_ref, oq_ref, os_ref, on_ref, h_ref):
    j = pl.program_id(1)

    @pl.when(j == 0)
    def _():
        h = _norm_mod(x_ref[...], g_ref[...], sh_ref[...], sc_ref[...]).astype(jnp.bfloat16)
        h_ref[...] = h
        small = jnp.dot(h, ws_ref[...], preferred_element_type=jnp.float32)
        nq = IDX_HEADS * IDX_DIM
        oq_ref[...] = small[:, :nq].T.astype(jnp.bfloat16)
        os_ref[...] = small[:, nq:]
        on_ref[...] = jnp.zeros(on_ref.shape, jnp.float32)

    y = jnp.dot(h_ref[...], wm_ref[...], preferred_element_type=jnp.float32)
    om_ref[...] = y.astype(jnp.bfloat16)

    for col, lane0 in ((FOX_Q_COL, 0), (FOX_K_COL, N_HEADS)):
        @pl.when(j == col)
        def _():
            sq = y * y
            on_ref[:, lane0:lane0 + N_HEADS] = jnp.concatenate(
                [jnp.sum(sq[:, h * HEAD_DIM:(h + 1) * HEAD_DIM], axis=-1, keepdims=True)
                 for h in range(N_HEADS)], axis=1)


def _inproj(x, g, shift, scale, w_main, w_small):
    s, d = x.shape
    n_main = w_main.shape[1]
    tm, tn = ROW_TILE, GROUP_WIDTH
    nq = IDX_HEADS * IDX_DIM
    row = lambda i, j: (0, 0)
    return pl.pallas_call(
        _inproj_kernel,
        grid=(s // tm, n_main // tn),
        in_specs=[pl.BlockSpec((tm, d), lambda i, j: (i, 0)),
                  pl.BlockSpec((1, d), row), pl.BlockSpec((1, d), row), pl.BlockSpec((1, d), row),
                  pl.BlockSpec((d, tn), lambda i, j: (0, j)),
                  pl.BlockSpec((d, SMALL_WIDTH), row)],
        out_specs=[pl.BlockSpec((tm, tn), lambda i, j: (i, j)),
                   pl.BlockSpec((nq, tm), lambda i, j: (0, i)),
                   pl.BlockSpec((tm, SMALL_WIDTH - nq), lambda i, j: (i, 0)),
                   pl.BlockSpec((tm, LANES), lambda i, j: (i, 0))],
        out_shape=[jax.ShapeDtypeStruct((s, n_main), jnp.bfloat16),
                   jax.ShapeDtypeStruct((nq, s), jnp.bfloat16),
                   jax.ShapeDtypeStruct((s, SMALL_WIDTH - nq), jnp.float32),
                   jax.ShapeDtypeStruct((s, LANES), jnp.float32)],
        scratch_shapes=[pltpu.VMEM((tm, d), jnp.bfloat16)],
        compiler_params=_cparams(("parallel", "arbitrary")),
        name="inproj",
    )(x, g, shift, scale, w_main, w_small)


def _fcum_kernel(fb_ref, b_ref, o_ref):
    r = fb_ref.shape[1]
    hi = lax.Precision.HIGHEST
    ci = lax.broadcasted_iota(jnp.int32, (LANES, LANES), 0)
    cj = lax.broadcasted_iota(jnp.int32, (LANES, LANES), 1)
    upper = (ci <= cj).astype(jnp.float32)
    ri = lax.broadcasted_iota(jnp.int32, (r, r), 0)
    rj = lax.broadcasted_iota(jnp.int32, (r, r), 1)
    strict_lower = (rj < ri).astype(jnp.float32)
    for h in range(N_HEADS):
        z = fb_ref[h] + b_ref[h]
        lf = jnp.minimum(z, 0.0) - jnp.log(1.0 + jnp.exp(-jnp.abs(z)))
        within = jnp.dot(lf, upper, precision=hi, preferred_element_type=jnp.float32)
        before = jnp.dot(strict_lower, within, precision=hi, preferred_element_type=jnp.float32)
        o_ref[h] = within + before[:, LANES - 1:LANES]


def _fcum(fb_t, b_forget):
    h, r, _ = fb_t.shape
    return pl.pallas_call(
        _fcum_kernel,
        in_specs=[pl.BlockSpec(memory_space=pltpu.VMEM),
                  pl.BlockSpec(memory_space=pltpu.SMEM)],
        out_specs=pl.BlockSpec(memory_space=pltpu.VMEM),
        out_shape=jax.ShapeDtypeStruct((h, r, LANES), jnp.float32),
        name="forget_cumsum",
    )(fb_t, b_forget)


def _ordered_bits(v):
    return v ^ ((v >> 31) & jnp.int32(0x7FFFFFFF))


def _float_to_key(x):
    return _ordered_bits(pltpu.bitcast(x, jnp.int32))


def _key_to_float(key):
    return pltpu.bitcast(_ordered_bits(key), jnp.float32)


def _fold_rows(x, op):
    parts = [x[r:r + SUBLANES] for r in range(0, x.shape[0], SUBLANES)]
    while len(parts) > 1:
        parts = [op(parts[k], parts[k + 1]) for k in range(0, len(parts) - 1, 2)] + (
            [parts[-1]] if len(parts) % 2 else [])
    return parts[0]


def _select_kernel(k_ref, qit_ref, wt_ref, mask_ref, s_ref, top_ref, short_ref, *, seq, n_sel):
    tq = qit_ref.shape[1]
    ck = SEL_CHUNK
    i = pl.program_id(0)
    n_chunks = ((i + 1) * tq + ck - 1) // ck
    n_rest = (seq - n_chunks * ck).astype(jnp.float32)
    q_pos = i * tq + lax.broadcasted_iota(jnp.int32, (SEL_ROWS, tq), 1)
    key_iota = lax.broadcasted_iota(jnp.int32, (SEL_ROWS, tq), 0)
    wts = wt_ref[...] * (IDX_HEADS ** -0.5)
    n_slots = top_ref.shape[0]

    top_ref[...] = jnp.full(top_ref.shape, -jnp.inf, jnp.float32)

    def score_chunk(c, signs):
        pos, nonneg = signs
        for r in range(0, ck, SEL_ROWS):
            r0 = pl.multiple_of(c * ck + r, SEL_ROWS)
            kc = k_ref[pl.ds(r0, SEL_ROWS), :]
            acc = jnp.zeros((SEL_ROWS, tq), jnp.float32)
            for h in range(IDX_HEADS):
                lg = jnp.dot(kc, qit_ref[h * IDX_DIM:(h + 1) * IDX_DIM, :],
                             preferred_element_type=jnp.float32)
                acc = acc + jnp.maximum(lg, 0.0) * wts[h:h + 1, :]
            sc = jnp.where(r0 + key_iota <= q_pos, acc, NEG)
            s_ref[pl.ds(r0, SEL_ROWS), :] = sc
            slot = r % n_slots
            top_ref[slot:slot + SEL_ROWS, :] = jnp.maximum(top_ref[slot:slot + SEL_ROWS, :], sc)
            pos = pos + _fold_rows(jnp.where(sc > 0.0, 1.0, 0.0), jnp.add)
            nonneg = nonneg + _fold_rows(jnp.where(sc >= 0.0, 1.0, 0.0), jnp.add)
        return pos, nonneg

    zero_count = jnp.zeros((SUBLANES, tq), jnp.float32)
    pos, nonneg = lax.fori_loop(0, n_chunks, score_chunk, (zero_count, zero_count))
    n_pos = jnp.sum(pos, axis=0, keepdims=True)
    n_nonneg = jnp.sum(nonneg, axis=0, keepdims=True)

    def count(pred, ref=s_ref, chunks=n_chunks):
        def body(c, accs):
            for r in range(0, ck, COUNT_ROWS):
                x = ref[pl.ds(pl.multiple_of(c * ck + r, COUNT_ROWS), COUNT_ROWS), :]
                accs = tuple(a + _fold_rows(jnp.where(f(x), 1.0, 0.0), jnp.add) for a, f in zip(accs, pred))
            return accs
        zero = jnp.zeros((SUBLANES, tq), jnp.float32)
        accs = lax.fori_loop(0, chunks, body, (zero,) * len(pred))
        return [jnp.sum(a, axis=0, keepdims=True) for a in accs]

    def count_ge(cf):
        return count([lambda x: x >= cf])[0] + jnp.where(NEG >= cf, n_rest, 0.0)

    top = top_ref[...]
    low_key = _float_to_key(jnp.min(_fold_rows(top, jnp.minimum), axis=0, keepdims=True))
    max_key = _float_to_key(jnp.max(_fold_rows(top, jnp.maximum), axis=0, keepdims=True))
    hi0 = jnp.maximum(max_key, _float_to_key(jnp.full((1, tq), NEG, jnp.float32))) + 1
    lo0 = jnp.full((1, tq), -2 ** 31, jnp.int32)
    max_passes = 96

    at_zero = jnp.logical_and(n_pos < n_sel, n_nonneg >= n_sel)
    above_zero = n_pos >= n_sel
    below_zero = n_nonneg < n_sel
    zero_key = jnp.zeros((1, tq), jnp.int32)
    lo0 = jnp.where(below_zero, lo0, zero_key)
    c_lo0 = jnp.where(below_zero, float(seq), n_nonneg)
    hi0 = jnp.where(above_zero, hi0, jnp.where(at_zero, zero_key + 2 ** 23, zero_key - 1))
    c_hi0 = jnp.where(above_zero, 0.0, jnp.where(at_zero, n_pos, n_nonneg))
    done0 = jnp.where(at_zero, 1, 0)

    def searching(st):
        p, _, _, _, _, done = st
        return jnp.logical_and(p < max_passes, jnp.min(done) == 0)

    def search_pass(count_fn, st):
        p, lo, hi, c_lo, c_hi, done = st
        mid_val = _float_to_key(0.5 * _key_to_float(lo) + 0.5 * _key_to_float(hi))
        mid_key = (lo >> 1) + (hi >> 1) + (lo & hi & 1)
        cand = jnp.where(p == 0, low_key, jnp.where((p & 3) == 0, mid_key, mid_val))
        cand = jnp.minimum(jnp.maximum(cand, lo + 1), hi - 1)
        cnt = count_fn(_key_to_float(cand))
        active = done == 0
        up = jnp.logical_and(active, cnt >= n_sel)
        down = jnp.logical_and(active, cnt < n_sel)
        lo, c_lo = jnp.where(up, cand, lo), jnp.where(up, cnt, c_lo)
        hi, c_hi = jnp.where(down, cand, hi), jnp.where(down, cnt, c_hi)
        finished = jnp.logical_and(active, jnp.logical_or(cnt == n_sel, hi - 1 <= lo))
        return p + 1, lo, hi, c_lo, c_hi, jnp.where(finished, 1, done)

    def narrowing(st):
        p, _, _, c_lo, c_hi, done = st
        wide = jnp.logical_and(done == 0, c_lo - c_hi > SHORTLIST)
        return jnp.logical_and(p < max_passes, jnp.max(jnp.where(wide, 1, 0)) > 0)

    st = lax.while_loop(narrowing, functools.partial(search_pass, count_ge),
                        (jnp.int32(0), lo0, hi0, c_lo0, c_hi0, done0))

    p1, lo1, hi1, c_lo1, c_hi1, done1 = st
    hi_f = _key_to_float(hi1)
    short_ref[...] = jnp.full(short_ref.shape, -jnp.inf, jnp.float32)

    def shortlist_chunk(c, _):
        for r in range(0, ck, COUNT_ROWS):
            x = s_ref[pl.ds(pl.multiple_of(c * ck + r, COUNT_ROWS), COUNT_ROWS), :]
            y = jnp.where(x < hi_f, x, -jnp.inf)
            first = short_ref[r:r + COUNT_ROWS, :]
            second = short_ref[ck + r:ck + r + COUNT_ROWS, :]
            short_ref[r:r + COUNT_ROWS, :] = jnp.maximum(first, y)
            short_ref[ck + r:ck + r + COUNT_ROWS, :] = jnp.maximum(second, jnp.minimum(first, y))
        return 0

    lax.fori_loop(0, n_chunks, shortlist_chunk, 0)
    lo_f = _key_to_float(lo1)
    listed = count([lambda x: x >= lo_f], short_ref, short_ref.shape[0] // ck)[0]
    complete = listed == c_lo1 - c_hi1

    def count_short(cf):
        return c_hi1 + count([lambda x: x >= cf], short_ref, short_ref.shape[0] // ck)[0]

    st = lax.while_loop(searching, functools.partial(search_pass, count_short),
                        (p1, lo1, hi1, c_lo1, c_hi1, jnp.where(complete, done1, 1)))
    p2, lo2, hi2, c_lo2, c_hi2, done2 = st
    pick = lambda a, b: jnp.where(complete, a, b)

    st = lax.while_loop(searching, functools.partial(search_pass, count_ge),
                        (p2, pick(lo2, lo1), pick(hi2, hi1), pick(c_lo2, c_lo1), pick(c_hi2, c_hi1),
                         pick(done2, done1)))
    _, lo, _, c_lo, c_hi, _ = st
    thr = _key_to_float(lo)

    need = jnp.where(c_lo > n_sel, n_sel - c_hi, float(seq))
    rt = RANK_TILE
    earlier = (lax.broadcasted_iota(jnp.int32, (rt, rt), 1)
               < lax.broadcasted_iota(jnp.int32, (rt, rt), 0)).astype(jnp.bfloat16)

    def mask_chunk(c, ties_before):
        for t0 in range(0, ck, rt):
            c0 = pl.multiple_of(c * ck + t0, rt)
            tie = jnp.where(s_ref[pl.ds(c0, rt), :] == thr, 1.0, 0.0)
            rank = ties_before + jnp.dot(earlier, tie.astype(jnp.bfloat16),
                                         preferred_element_type=jnp.float32)
            for r in range(0, rt, SEL_ROWS):
                r0 = pl.multiple_of(c0 + r, SEL_ROWS)
                x = s_ref[pl.ds(r0, SEL_ROWS), :]
                kept_tie = jnp.logical_and(x == thr, rank[r:r + SEL_ROWS] < need)
                keep = jnp.logical_and(jnp.logical_or(x > thr, kept_tie), r0 + key_iota <= q_pos)
                mask_ref[:, pl.ds(r0, SEL_ROWS)] = jnp.where(keep, 0.0, NEG_MASK).T.astype(jnp.bfloat16)
            ties_before = ties_before + jnp.sum(_fold_rows(tie, jnp.add), axis=0, keepdims=True)
        return ties_before

    lax.fori_loop(0, n_chunks, mask_chunk, jnp.zeros((1, tq), jnp.float32))

    def fill_chunk(c, _):
        r0 = pl.multiple_of(c * ck, ck)
        mask_ref[:, pl.ds(r0, ck)] = jnp.full((tq, ck), NEG_MASK, jnp.bfloat16)
        return 0

    lax.fori_loop(n_chunks, seq // ck, fill_chunk, 0)


def _select(k_idx, qi_t, w_t, n_sel):
    s = k_idx.shape[0]
    tq = SEL_TILE
    assert n_sel <= RANK_TILE and s % SEL_CHUNK == 0 and SEL_CHUNK % RANK_TILE == 0
    return pl.pallas_call(
        functools.partial(_select_kernel, seq=s, n_sel=n_sel),
        grid=(s // tq,),
        in_specs=[pl.BlockSpec(k_idx.shape, lambda i: (0, 0)),
                  pl.BlockSpec((qi_t.shape[0], tq), lambda i: (0, i)),
                  pl.BlockSpec((IDX_HEADS, tq), lambda i: (0, i))],
        out_specs=pl.BlockSpec((tq, s), lambda i: (i, 0)),
        out_shape=jax.ShapeDtypeStruct((s, s), jnp.bfloat16),
        scratch_shapes=[pltpu.VMEM((s, tq), jnp.float32), pltpu.VMEM((RANK_TILE, tq), jnp.float32),
                        pltpu.VMEM((2 * SEL_CHUNK, tq), jnp.float32)],
        compiler_params=_cparams(("parallel",)),
        name="dsa_select",
    )(k_idx, qi_t, w_t)


def _qk(a, b):
    return lax.dot_general(a, b, (((1,), (1,)), ((), ())), preferred_element_type=jnp.float32)


def _pair_tables(nq):
    qt = np.concatenate([np.full(q + 1, q, np.int32) for q in range(nq)])
    st = np.concatenate([np.arange(q + 1, dtype=np.int32) for q in range(nq)])
    return jnp.asarray(qt), jnp.asarray(st)


def _softmax_step(h, s, v, m_ref, acc_ref):
    ts = s.shape[1]
    m_prev = m_ref[h]
    m_new = jnp.maximum(m_prev, jnp.max(s, axis=-1, keepdims=True))
    alpha = jnp.exp2(m_prev - m_new)
    pm = jnp.exp2(s - jnp.tile(m_new, (1, ts // LANES))).astype(jnp.bfloat16)
    v1 = jnp.concatenate([v, jnp.ones((ts, HEAD_DIM), jnp.bfloat16)], axis=1)
    acc_ref[h] = (jnp.tile(alpha, (1, 2 * HEAD_DIM // LANES)) * acc_ref[h]
                  + jnp.dot(pm, v1, preferred_element_type=jnp.float32))
    m_ref[h] = m_new


def _init_softmax(m_ref, acc_ref):
    m_ref[...] = jnp.full(m_ref.shape, NEG_MASK, jnp.float32)
    acc_ref[...] = jnp.zeros(acc_ref.shape, jnp.float32)


def _softmax_result(h, acc_ref):
    return acc_ref[h, :, :HEAD_DIM] / acc_ref[h, :, HEAD_DIM:]


def _att_scratch(tq):
    return [pltpu.VMEM((N_HEADS, tq, LANES), jnp.float32),
            pltpu.VMEM((N_HEADS, tq, 2 * HEAD_DIM), jnp.float32)]


def _dsa_kernel(qt_ref, st_ref, q_ref, k_ref, v_ref, mask_ref, bias_ref, o_ref, m_ref, acc_ref):
    p = pl.program_id(0)
    qi, si = qt_ref[p], st_ref[p]
    tq, ts = mask_ref.shape
    bt = bias_ref.shape[-1]
    lag00 = (qi * tq - si * ts) // bt
    lags = [[lag00 + a - b for b in range(ts // bt)] for a in range(tq // bt)]
    quad = [[jnp.where(lag == 0, 0, jnp.where(lag == 1, 1, 2)) for lag in row] for row in lags]
    near = lags[0][-1] <= 1
    last = (si + 1) * ts >= (qi + 1) * tq

    @pl.when(si == 0)
    def _():
        _init_softmax(m_ref, acc_ref)

    def sweep(with_bias):
        maskf = mask_ref[...].astype(jnp.float32)
        for h in range(N_HEADS):
            hs = slice(h * HEAD_DIM, (h + 1) * HEAD_DIM)
            s = _qk(q_ref[:, hs], k_ref[:, hs]) + maskf
            if with_bias:
                s = s + jnp.concatenate(
                    [jnp.concatenate([bias_ref[h, idx] for idx in row], axis=1) for row in quad], axis=0)
            _softmax_step(h, s, v_ref[:, hs], m_ref, acc_ref)

    pl.when(near)(functools.partial(sweep, True))
    pl.when(jnp.logical_not(near))(functools.partial(sweep, False))

    @pl.when(last)
    def _():
        for h in range(N_HEADS):
            hs = slice(h * HEAD_DIM, (h + 1) * HEAD_DIM)
            o_ref[:, hs] = _softmax_result(h, acc_ref).astype(o_ref.dtype)


def _dsa(proj, mask, bias_tiles):
    s = proj.shape[0]
    tq, ts = DSA_Q_TILE, ATT_TILE
    assert tq % ts == 0 and ts % BIAS_TILE == 0 and bias_tiles.shape[1:] == (3, BIAS_TILE, BIAS_TILE)
    r = tq // ts
    nq = s // tq
    qt = jnp.asarray(np.concatenate([np.full(r * (q + 1), q, np.int32) for q in range(nq)]))
    st = jnp.asarray(np.concatenate([np.arange(r * (q + 1), dtype=np.int32) for q in range(nq)]))
    w = GROUP_WIDTH
    grid_spec = pltpu.PrefetchScalarGridSpec(
        num_scalar_prefetch=2,
        grid=(qt.shape[0],),
        in_specs=[pl.BlockSpec((tq, w), lambda p, qt, st: (qt[p], 0)),
                  pl.BlockSpec((ts, w), lambda p, qt, st: (st[p], 1)),
                  pl.BlockSpec((ts, w), lambda p, qt, st: (st[p], 2)),
                  pl.BlockSpec((tq, ts), lambda p, qt, st: (qt[p], st[p])),
                  pl.BlockSpec(bias_tiles.shape, lambda p, qt, st: (0, 0, 0, 0))],
        out_specs=pl.BlockSpec((tq, w), lambda p, qt, st: (qt[p], 0)),
        scratch_shapes=_att_scratch(tq))
    return pl.pallas_call(
        _dsa_kernel, grid_spec=grid_spec,
        out_shape=jax.ShapeDtypeStruct((s, w), jnp.bfloat16),
        compiler_params=_cparams(("arbitrary",)),
        name="dsa_attention",
    )(qt, st, proj, proj, proj, mask, bias_tiles)


def _fox_kernel(qt_ref, st_ref, se_ref, live_ref, q_ref, k_ref, v_ref, g_ref, fq_ref, fs_ref, o_ref,
                m_ref, acc_ref):
    p = pl.program_id(0)
    qi, si = qt_ref[p], st_ref[p]
    tq, ts = q_ref.shape[0], k_ref.shape[0]

    @pl.when(si == 0)
    def _():
        _init_softmax(m_ref, acc_ref)

    def sweep(diagonal):
        if diagonal:
            causal = (lax.broadcasted_iota(jnp.int32, (tq, ts), 1)
                      <= lax.broadcasted_iota(jnp.int32, (tq, ts), 0))
        for h in range(N_HEADS):
            hs = slice(h * HEAD_DIM, (h + 1) * HEAD_DIM)
            decay = (fq_ref[h:h + 1, 0:1] - fs_ref[h:h + 1, :]) * LOG2E
            s = _qk(q_ref[:, hs], k_ref[:, hs]) + decay
            if diagonal:
                s = jnp.where(causal, s, NEG_MASK)
            _softmax_step(h, s, v_ref[:, hs], m_ref, acc_ref)

    pl.when(si == qi)(functools.partial(sweep, True))
    pl.when(jnp.logical_and(si != qi, live_ref[p] != 0))(functools.partial(sweep, False))

    @pl.when(si == qi)
    def _():
        for h in range(N_HEADS):
            hs = slice(h * HEAD_DIM, (h + 1) * HEAD_DIM)
            g = g_ref[:, hs].astype(jnp.float32)
            o_ref[:, hs] = (_softmax_result(h, acc_ref) / (1.0 + jnp.exp(-g))).astype(o_ref.dtype)


def _fox_schedule(sq_norms, f_rows, qt, st, t):
    n = sq_norms.shape[0] // t
    tile_norm = jnp.sqrt(jnp.max(sq_norms[:, :2 * N_HEADS].reshape(n, t, 2 * N_HEADS), axis=1)).T
    qn, kn = tile_norm[:N_HEADS], tile_norm[N_HEADS:]
    f = f_rows.reshape(N_HEADS, n, t)
    f_min, f_max = jnp.min(f, axis=-1), jnp.max(f, axis=-1)
    bound = (NORM_SLACK * qn[:, :, None] * (kn[:, None, :] + kn[:, :, None])
             - (f_min[:, None, :] - f_max[:, :, None]) * LOG2E)
    live_tiles = jnp.any(bound > -SKIP_LOG2, axis=0)
    live = jnp.logical_or(live_tiles[qt, st], qt == st)
    steps = jnp.arange(qt.shape[0], dtype=jnp.int32)
    next_live = lax.cummin(jnp.where(live, steps, qt.shape[0]), axis=0, reverse=True)
    return st[next_live], live.astype(jnp.int32)


def _fox(proj, sq_norms, f_rows):
    s = proj.shape[0]
    t = ATT_TILE
    qt, st = _pair_tables(s // t)
    se, live = _fox_schedule(sq_norms, f_rows, qt, st, t)
    w = GROUP_WIDTH
    grid_spec = pltpu.PrefetchScalarGridSpec(
        num_scalar_prefetch=4,
        grid=(qt.shape[0],),
        in_specs=[pl.BlockSpec((t, w), lambda p, qt, st, se, lv: (qt[p], 3)),
                  pl.BlockSpec((t, w), lambda p, qt, st, se, lv: (se[p], 4)),
                  pl.BlockSpec((t, w), lambda p, qt, st, se, lv: (se[p], 5)),
                  pl.BlockSpec((t, w), lambda p, qt, st, se, lv: (qt[p], 6)),
                  pl.BlockSpec((N_HEADS, t), lambda p, qt, st, se, lv: (0, qt[p])),
                  pl.BlockSpec((N_HEADS, t), lambda p, qt, st, se, lv: (0, se[p]))],
        out_specs=pl.BlockSpec((t, w), lambda p, qt, st, se, lv: (qt[p], 0)),
        scratch_shapes=_att_scratch(t))
    return pl.pallas_call(
        _fox_kernel, grid_spec=grid_spec,
        out_shape=jax.ShapeDtypeStruct((s, w), jnp.bfloat16),
        compiler_params=_cparams(("arbitrary",)),
        name="fox_attention",
    )(qt, st, se, live, proj, proj, proj, proj, f_rows, f_rows)


def _outproj_kernel(x_ref, oa_ref, ob_ref, wa_ref, wb_ref, gate_ref, o_ref):
    y = jnp.dot(oa_ref[...], wa_ref[...], preferred_element_type=jnp.float32)
    y = y + jnp.dot(ob_ref[...], wb_ref[...], preferred_element_type=jnp.float32)
    o_ref[...] = x_ref[...] + gate_ref[...] * y


def _outproj(x, o_a, o_b, w_a, w_b, gate):
    s, d = x.shape
    tm, tn = ROW_TILE, 1024
    w = GROUP_WIDTH
    return pl.pallas_call(
        _outproj_kernel,
        grid=(s // tm, d // tn),
        in_specs=[pl.BlockSpec((tm, tn), lambda i, j: (i, j)),
                  pl.BlockSpec((tm, w), lambda i, j: (i, 0)),
                  pl.BlockSpec((tm, w), lambda i, j: (i, 0)),
                  pl.BlockSpec((w, tn), lambda i, j: (0, j)),
                  pl.BlockSpec((w, tn), lambda i, j: (0, j)),
                  pl.BlockSpec((1, tn), lambda i, j: (0, j))],
        out_specs=pl.BlockSpec((tm, tn), lambda i, j: (i, j)),
        out_shape=jax.ShapeDtypeStruct((s, d), jnp.float32),
        compiler_params=_cparams(("parallel", "arbitrary")),
        name="outproj",
    )(x, o_a, o_b, w_a, w_b, gate)


def _conv(y_ref, cw_ref, cb_ref, tm):
    h = CONV_HALO
    return (cw_ref[2:3, :] * y_ref[h:h + tm, :] + cw_ref[1:2, :] * y_ref[h - 1:h - 1 + tm, :]
            + cw_ref[0:1, :] * y_ref[h - 2:h - 2 + tm, :] + cb_ref[...])


def _ffn_kernel(x_ref, xh_ref, g_ref, sh_ref, sc_ref, wg_ref, wv_ref, cwg_ref, cwv_ref, cbg_ref, cbv_ref,
                wd_ref, gate_ref, gf_ref, o_ref, h_ref, yg_ref, yv_ref, acc_ref):
    i, f = pl.program_id(0), pl.program_id(1)
    tm = x_ref.shape[0]

    @pl.when(f == 0)
    def _():
        halo = _norm_mod(xh_ref[...], g_ref[...], sh_ref[...], sc_ref[...])
        h_ref[0:CONV_HALO, :] = jnp.where(i > 0, halo, 0.0).astype(jnp.bfloat16)
        h_ref[CONV_HALO:, :] = _norm_mod(x_ref[...], g_ref[...], sh_ref[...], sc_ref[...]).astype(jnp.bfloat16)
        acc_ref[...] = jnp.zeros(acc_ref.shape, jnp.float32)

    hb = h_ref[...]
    yg_ref[...] = jnp.dot(hb, wg_ref[...], preferred_element_type=jnp.float32)
    yv_ref[...] = jnp.dot(hb, wv_ref[...], preferred_element_type=jnp.float32)
    ug = _conv(yg_ref, cwg_ref, cbg_ref, tm)
    uv = _conv(yv_ref, cwv_ref, cbv_ref, tm)
    a = (ug / (1.0 + jnp.exp(-ug))) * uv
    acc_ref[...] += jnp.dot(a.astype(jnp.bfloat16), wd_ref[...], preferred_element_type=jnp.float32)

    @pl.when(f == pl.num_programs(1) - 1)
    def _():
        x2 = x_ref[...] + gate_ref[...] * acc_ref[...]
        y = x2 * lax.rsqrt(jnp.mean(x2 * x2, axis=-1, keepdims=True) + EPS)
        o_ref[...] = y * gf_ref[...]


def _ffn(x, g, shift, scale, w_up, conv_w, conv_b, w_down, gate, g_final):
    s, d = x.shape
    fp = w_down.shape[0]
    tm, tf = ROW_TILE, FF_TILE
    nf = fp // tf
    hb = tm // CONV_HALO
    row = lambda i, f: (0, 0)
    return pl.pallas_call(
        _ffn_kernel,
        grid=(s // tm, nf),
        in_specs=[pl.BlockSpec((tm, d), lambda i, f: (i, 0)),
                  pl.BlockSpec((CONV_HALO, d), lambda i, f: (jnp.maximum(i * hb - 1, 0), 0)),
                  pl.BlockSpec((1, d), row), pl.BlockSpec((1, d), row), pl.BlockSpec((1, d), row),
                  pl.BlockSpec((d, tf), lambda i, f: (0, f)),
                  pl.BlockSpec((d, tf), lambda i, f: (0, f + nf)),
                  pl.BlockSpec((CONV_WIDTH, tf), lambda i, f: (0, f)),
                  pl.BlockSpec((CONV_WIDTH, tf), lambda i, f: (0, f + nf)),
                  pl.BlockSpec((1, tf), lambda i, f: (0, f)),
                  pl.BlockSpec((1, tf), lambda i, f: (0, f + nf)),
                  pl.BlockSpec((tf, d), lambda i, f: (f, 0)),
                  pl.BlockSpec((1, d), row), pl.BlockSpec((1, d), row)],
        out_specs=pl.BlockSpec((tm, d), lambda i, f: (i, 0)),
        out_shape=jax.ShapeDtypeStruct((s, d), jnp.float32),
        scratch_shapes=[pltpu.VMEM((tm + CONV_HALO, d), jnp.bfloat16),
                        pltpu.VMEM((tm + CONV_HALO, tf), jnp.float32),
                        pltpu.VMEM((tm + CONV_HALO, tf), jnp.float32),
                        pltpu.VMEM((tm, d), jnp.float32)],
        compiler_params=_cparams(("parallel", "arbitrary")),
        name="conv_ffn",
    )(x, x, g, shift, scale, w_up, w_up, conv_w, conv_w, conv_b, conv_b, w_down, gate, g_final)


def _t5_bias_tiles(rel_bias, t):
    max_exact = N_BUCKETS // 2
    d = jnp.arange(2 * t, dtype=jnp.int32)
    df = jnp.maximum(d, 1).astype(jnp.float32)
    large = max_exact + (jnp.log(df / max_exact) / math.log(MAX_DISTANCE / max_exact)
                         * (N_BUCKETS - max_exact)).astype(jnp.int32)
    bucket = jnp.where(d < max_exact, d, jnp.minimum(large, N_BUCKETS - 1))
    rb = rel_bias.astype(jnp.float32)
    table = (rb[bucket] - rb[N_BUCKETS - 1]) * LOG2E
    period = 2 * t
    k = np.arange(period)
    tiles = []
    for off in (0, t):
        idx = np.clip(np.where(k < t, off - k, off + period - k), 0, 2 * t - 1)
        seq = jnp.tile(table[idx].T, (1, t))
        tiles.append(seq[:, :t * (period - 1)].reshape(N_HEADS, t, period - 1)[:, :, :t])
    return jnp.stack(tiles + [jnp.zeros_like(tiles[0])], axis=1)


def _pad_cols(a, n):
    return jnp.pad(a, ((0, 0), (0, n - a.shape[1])))


def kernel(x, c, rel_bias, w_ada, b_ada, g_attn, w_in, b_forget, w_out, g_mlp, w_up, conv_w, conv_b,
           w_down, g_final):
    b, s, d = x.shape
    assert b == 1 and s % ROW_TILE == 0 and s % DSA_Q_TILE == 0 and s % SEL_TILE == 0
    assert BIAS_TILE >= MAX_DISTANCE
    depth = w_ada.shape[0]
    assert depth == 1, "the fused MLP kernel applies the final norm, so it must be the last layer"
    d_ff = w_down.shape[1]
    ff_pad = -(-d_ff // FF_TILE) * FF_TILE
    n_sel = min(TOPK_MAX, s // 4)
    gw, nq = GROUP_WIDTH, IDX_HEADS * IDX_DIM
    bf = jnp.bfloat16
    xs = x[0]
    c_col = c.reshape(d, 1)
    bias_tiles = _t5_bias_tiles(rel_bias, BIAS_TILE)

    for l in range(depth):
        mod = _ada(c_col, w_ada[l], b_ada[l][None, :])
        shift_a, scale_a, gate_a, shift_m, scale_m, gate_m = [mod[:, k * d:(k + 1) * d] for k in range(6)]

        wi = w_in[l]
        o = np.cumsum([0, gw, gw, gw, nq, IDX_DIM, IDX_HEADS, gw, gw, gw, gw, N_HEADS])
        seg = lambda k: wi[:, o[k]:o[k + 1]]
        q_scale = HEAD_DIM ** -0.5 * LOG2E
        w_main = jnp.concatenate([seg(0) * q_scale, seg(1), seg(2),
                                  seg(6) * q_scale, seg(7), seg(8), seg(9)], axis=1).astype(bf)
        w_small = _pad_cols(jnp.concatenate([seg(3) * IDX_DIM ** -0.5, seg(4), seg(5), seg(10)], axis=1),
                            SMALL_WIDTH).astype(bf)

        proj, qi_t, small, sq_norms = _inproj(xs, g_attn[l][None, :], shift_a, scale_a, w_main, w_small)

        k_idx = small[:, :IDX_DIM].astype(bf)
        w_t = small[:, IDX_DIM:IDX_DIM + IDX_HEADS].T
        fb_t = small[:, IDX_DIM + IDX_HEADS:IDX_DIM + IDX_HEADS + N_HEADS].T
        f_rows = _fcum(fb_t.reshape(N_HEADS, s // LANES, LANES), b_forget[l]).reshape(N_HEADS, s)

        mask = _select(k_idx, qi_t, w_t, n_sel)
        o_a = _dsa(proj, mask, bias_tiles)
        o_b = _fox(proj, sq_norms, f_rows)

        wo = w_out[l].astype(bf)
        x1 = _outproj(xs, o_a, o_b, wo[:gw], wo[gw:], gate_a)

        wu = w_up[l].astype(bf)
        w_up_p = jnp.concatenate([_pad_cols(wu[:, :d_ff], ff_pad), _pad_cols(wu[:, d_ff:], ff_pad)], axis=1)
        cw = conv_w[l]
        cw_p = jnp.concatenate([_pad_cols(cw[:, :d_ff], ff_pad), _pad_cols(cw[:, d_ff:], ff_pad)], axis=1)
        cb = conv_b[l][None, :]
        cb_p = jnp.concatenate([_pad_cols(cb[:, :d_ff], ff_pad), _pad_cols(cb[:, d_ff:], ff_pad)], axis=1)
        w_down_p = jnp.pad(w_down[l].astype(bf), ((0, ff_pad - d_ff), (0, 0)))

        xs = _ffn(x1, g_mlp[l][None, :], shift_m, scale_m, w_up_p, cw_p, cb_p, w_down_p, gate_m,
                  g_final[None, :])

    return xs[None]
```

```python
import functools
import math

import jax
import jax.numpy as jnp
import numpy as np
from jax import lax
from jax.experimental import pallas as pl
from jax.experimental.pallas import tpu as pltpu

HEAD_DIM = 128
N_HEADS = 8
GROUP_WIDTH = N_HEADS * HEAD_DIM
IDX_HEADS = 8
IDX_DIM = 64
TOPK_MAX = 256
N_BUCKETS = 32
MAX_DISTANCE = 128
CONV_WIDTH = 3
EPS = 1e-6
NEG = -1e30
NEG_MASK = -(2.0 ** 100)

LANES = 128
SUBLANES = 8
VMEM_LIMIT = 56 * 1024 * 1024

ROW_TILE = 512
ATT_TILE = 512
DSA_Q_TILE = 1024
SEL_TILE = 256
BIAS_TILE = 256
LOG2E = math.log2(math.e)
SEL_ROWS = 128
SEL_CHUNK = 512
COUNT_ROWS = 64
SHORTLIST = 12
SKIP_LOG2 = 160.0
NORM_SLACK = 1.01
FOX_Q_COL, FOX_K_COL = 3, 4
RANK_TILE = 256
FF_TILE = 512
CONV_HALO = 16
SMALL_WIDTH = 640


def _cparams(sem):
    return pltpu.CompilerParams(dimension_semantics=sem, vmem_limit_bytes=VMEM_LIMIT)


def _ada_kernel(c_ref, w_ref, b_ref, o_ref):
    d = w_ref.shape[0]
    rows = 256

    def body(r, acc):
        sl = pl.ds(pl.multiple_of(r * rows, rows), rows)
        cc = c_ref[sl, :]
        ca = cc / (1.0 + jnp.exp(-cc))
        prod = w_ref[sl, :] * ca
        return acc + prod.reshape(rows // SUBLANES, SUBLANES, -1).sum(axis=0)

    acc = lax.fori_loop(0, d // rows, body, jnp.zeros((SUBLANES, w_ref.shape[1]), jnp.float32))
    o_ref[...] = jnp.sum(acc, axis=0, keepdims=True) + b_ref[...]


def _ada(c_col, w, b_row):
    d, n = w.shape
    tn = 1024
    return pl.pallas_call(
        _ada_kernel,
        grid=(n // tn,),
        in_specs=[pl.BlockSpec((d, 1), lambda j: (0, 0)),
                  pl.BlockSpec((d, tn), lambda j: (0, j)),
                  pl.BlockSpec((1, tn), lambda j: (0, j))],
        out_specs=pl.BlockSpec((1, tn), lambda j: (0, j)),
        out_shape=jax.ShapeDtypeStruct((1, n), jnp.float32),
        compiler_params=_cparams(("arbitrary",)),
        name="adaln",
    )(c_col, w, b_row)


def _norm_mod(x, g, shift, scale):
    y = x * lax.rsqrt(jnp.mean(x * x, axis=-1, keepdims=True) + EPS)
    return (y * g) * (1.0 + scale) + shift


def _inproj_kernel(x_ref, g_ref, sh_ref, sc_ref, wm_ref, ws_ref, om_ref, oq_ref, os_ref, on_ref, h_ref):
    j = pl.program_id(1)

    @pl.when(j == 0)
    def _():
        h = _norm_mod(x_ref[...], g_ref[...], sh_ref[...], sc_ref[...]).astype(jnp.bfloat16)
        h_ref[...] = h
        small = jnp.dot(h, ws_ref[...], preferred_element_type=jnp.float32)
        nq = IDX_HEADS * IDX_DIM
        oq_ref[...] = small[:, :nq].astype(jnp.bfloat16)
        os_ref[...] = small[:, nq:]
        on_ref[...] = jnp.zeros(on_ref.shape, jnp.float32)

    y = jnp.dot(h_ref[...], wm_ref[...], preferred_element_type=jnp.float32)
    om_ref[...] = y.astype(jnp.bfloat16)

    for col, lane0 in ((FOX_Q_COL, 0), (FOX_K_COL, N_HEADS)):
        @pl.when(j == col)
        def _():
            sq = y * y
            on_ref[:, lane0:lane0 + N_HEADS] = jnp.concatenate(
                [jnp.sum(sq[:, h * HEAD_DIM:(h + 1) * HEAD_DIM], axis=-1, keepdims=True)
                 for h in range(N_HEADS)], axis=1)


def _inproj(x, g, shift, scale, w_main, w_small):
    s, d = x.shape
    n_main = w_main.shape[1]
    tm, tn = ROW_TILE, GROUP_WIDTH
    nq = IDX_HEADS * IDX_DIM
    row = lambda i, j: (0, 0)
    return pl.pallas_call(
        _inproj_kernel,
        grid=(s // tm, n_main // tn),
        in_specs=[pl.BlockSpec((tm, d), lambda i, j: (i, 0)),
                  pl.BlockSpec((1, d), row), pl.BlockSpec((1, d), row), pl.BlockSpec((1, d), row),
                  pl.BlockSpec((d, tn), lambda i, j: (0, j)),
                  pl.BlockSpec((d, SMALL_WIDTH), row)],
        out_specs=[pl.BlockSpec((tm, tn), lambda i, j: (i, j)),
                   pl.BlockSpec((tm, nq), lambda i, j: (i, 0)),
                   pl.BlockSpec((tm, SMALL_WIDTH - nq), lambda i, j: (i, 0)),
                   pl.BlockSpec((tm, LANES), lambda i, j: (i, 0))],
        out_shape=[jax.ShapeDtypeStruct((s, n_main), jnp.bfloat16),
                   jax.ShapeDtypeStruct((s, nq), jnp.bfloat16),
                   jax.ShapeDtypeStruct((s, SMALL_WIDTH - nq), jnp.float32),
                   jax.ShapeDtypeStruct((s, LANES), jnp.float32)],
        scratch_shapes=[pltpu.VMEM((tm, d), jnp.bfloat16)],
        compiler_params=_cparams(("parallel", "arbitrary")),
        name="inproj",
    )(x, g, shift, scale, w_main, w_small)


def _fcum_kernel(fb_ref, b_ref, o_ref):
    r = fb_ref.shape[1]
    hi = lax.Precision.HIGHEST
    ci = lax.broadcasted_iota(jnp.int32, (LANES, LANES), 0)
    cj = lax.broadcasted_iota(jnp.int32, (LANES, LANES), 1)
    upper = (ci <= cj).astype(jnp.float32)
    ri = lax.broadcasted_iota(jnp.int32, (r, r), 0)
    rj = lax.broadcasted_iota(jnp.int32, (r, r), 1)
    strict_lower = (rj < ri).astype(jnp.float32)
    for h in range(N_HEADS):
        z = fb_ref[h] + b_ref[h]
        lf = jnp.minimum(z, 0.0) - jnp.log(1.0 + jnp.exp(-jnp.abs(z)))
        within = jnp.dot(lf, upper, precision=hi, preferred_element_type=jnp.float32)
        before = jnp.dot(strict_lower, within, precision=hi, preferred_element_type=jnp.float32)
        o_ref[h] = within + before[:, LANES - 1:LANES]


def _fcum(fb_t, b_forget):
    h, r, _ = fb_t.shape
    return pl.pallas_call(
        _fcum_kernel,
        in_specs=[pl.BlockSpec(memory_space=pltpu.VMEM),
                  pl.BlockSpec(memory_space=pltpu.SMEM)],
        out_specs=pl.BlockSpec(memory_space=pltpu.VMEM),
        out_shape=jax.ShapeDtypeStruct((h, r, LANES), jnp.float32),
        name="forget_cumsum",
    )(fb_t, b_forget)


def _ordered_bits(v):
    return v ^ ((v >> 31) & jnp.int32(0x7FFFFFFF))


def _float_to_key(x):
    return _ordered_bits(pltpu.bitcast(x, jnp.int32))


def _key_to_float(key):
    return pltpu.bitcast(_ordered_bits(key), jnp.float32)


def _fold_rows(x, op):
    parts = [x[r:r + SUBLANES] for r in range(0, x.shape[0], SUBLANES)]
    while len(parts) > 1:
        parts = [op(parts[k], parts[k + 1]) for k in range(0, len(parts) - 1, 2)] + (
            [parts[-1]] if len(parts) % 2 else [])
    return parts[0]


def _select_kernel(k_ref, qit_ref, wt_ref, mask_ref, s_ref, top_ref, short_ref, *, seq, n_sel):
    tq = qit_ref.shape[1]
    ck = SEL_CHUNK
    i = pl.program_id(0)
    n_chunks = ((i + 1) * tq + ck - 1) // ck
    n_rest = (seq - n_chunks * ck).astype(jnp.float32)
    q_pos = i * tq + lax.broadcasted_iota(jnp.int32, (SEL_ROWS, tq), 1)
    key_iota = lax.broadcasted_iota(jnp.int32, (SEL_ROWS, tq), 0)
    wts = wt_ref[...] * (IDX_HEADS ** -0.5)
    n_slots = top_ref.shape[0]

    top_ref[...] = jnp.full(top_ref.shape, -jnp.inf, jnp.float32)

    def score_chunk(c, signs):
        pos, nonneg = signs
        for r in range(0, ck, SEL_ROWS):
            r0 = pl.multiple_of(c * ck + r, SEL_ROWS)
            kc = k_ref[pl.ds(r0, SEL_ROWS), :]
            acc = jnp.zeros((SEL_ROWS, tq), jnp.float32)
            for h in range(IDX_HEADS):
                lg = jnp.dot(kc, qit_ref[h * IDX_DIM:(h + 1) * IDX_DIM, :],
                             preferred_element_type=jnp.float32)
                acc = acc + jnp.maximum(lg, 0.0) * wts[h:h + 1, :]
            sc = jnp.where(r0 + key_iota <= q_pos, acc, NEG)
            s_ref[pl.ds(r0, SEL_ROWS), :] = sc
            slot = r % n_slots
            top_ref[slot:slot + SEL_ROWS, :] = jnp.maximum(top_ref[slot:slot + SEL_ROWS, :], sc)
            pos = pos + _fold_rows(jnp.where(sc > 0.0, 1.0, 0.0), jnp.add)
            nonneg = nonneg + _fold_rows(jnp.where(sc >= 0.0, 1.0, 0.0), jnp.add)
        return pos, nonneg

    zero_count = jnp.zeros((SUBLANES, tq), jnp.float32)
    pos, nonneg = lax.fori_loop(0, n_chunks, score_chunk, (zero_count, zero_count))
    n_pos = jnp.sum(pos, axis=0, keepdims=True)
    n_nonneg = jnp.sum(nonneg, axis=0, keepdims=True)

    def count(pred, ref=s_ref, chunks=n_chunks):
        def body(c, accs):
            for r in range(0, ck, COUNT_ROWS):
                x = ref[pl.ds(pl.multiple_of(c * ck + r, COUNT_ROWS), COUNT_ROWS), :]
                accs = tuple(a + _fold_rows(jnp.where(f(x), 1.0, 0.0), jnp.add) for a, f in zip(accs, pred))
            return accs
        zero = jnp.zeros((SUBLANES, tq), jnp.float32)
        accs = lax.fori_loop(0, chunks, body, (zero,) * len(pred))
        return [jnp.sum(a, axis=0, keepdims=True) for a in accs]

    def count_ge(cf):
        return count([lambda x: x >= cf])[0] + jnp.where(NEG >= cf, n_rest, 0.0)

    top = top_ref[...]
    low_key = _float_to_key(jnp.min(_fold_rows(top, jnp.minimum), axis=0, keepdims=True))
    max_key = _float_to_key(jnp.max(_fold_rows(top, jnp.maximum), axis=0, keepdims=True))
    hi0 = jnp.maximum(max_key, _float_to_key(jnp.full((1, tq), NEG, jnp.float32))) + 1
    lo0 = jnp.full((1, tq), -2 ** 31, jnp.int32)
    max_passes = 96

    at_zero = jnp.logical_and(n_pos < n_sel, n_nonneg >= n_sel)
    above_zero = n_pos >= n_sel
    below_zero = n_nonneg < n_sel
    zero_key = jnp.zeros((1, tq), jnp.int32)
    lo0 = jnp.where(below_zero, lo0, zero_key)
    c_lo0 = jnp.where(below_zero, float(seq), n_nonneg)
    hi0 = jnp.where(above_zero, hi0, jnp.where(at_zero, zero_key + 2 ** 23, zero_key - 1))
    c_hi0 = jnp.where(above_zero, 0.0, jnp.where(at_zero, n_pos, n_nonneg))
    done0 = jnp.where(at_zero, 1, 0)

    def searching(st):
        p, _, _, _, _, done = st
        return jnp.logical_and(p < max_passes, jnp.min(done) == 0)

    def search_pass(count_fn, st):
        p, lo, hi, c_lo, c_hi, done = st
        mid_val = _float_to_key(0.5 * _key_to_float(lo) + 0.5 * _key_to_float(hi))
        mid_key = (lo >> 1) + (hi >> 1) + (lo & hi & 1)
        cand = jnp.where(p == 0, low_key, jnp.where((p & 3) == 0, mid_key, mid_val))
        cand = jnp.minimum(jnp.maximum(cand, lo + 1), hi - 1)
        cnt = count_fn(_key_to_float(cand))
        active = done == 0
        up = jnp.logical_and(active, cnt >= n_sel)
        down = jnp.logical_and(active, cnt < n_sel)
        lo, c_lo = jnp.where(up, cand, lo), jnp.where(up, cnt, c_lo)
        hi, c_hi = jnp.where(down, cand, hi), jnp.where(down, cnt, c_hi)
        finished = jnp.logical_and(active, jnp.logical_or(cnt == n_sel, hi - 1 <= lo))
        return p + 1, lo, hi, c_lo, c_hi, jnp.where(finished, 1, done)

    def narrowing(st):
        p, _, _, c_lo, c_hi, done = st
        wide = jnp.logical_and(done == 0, c_lo - c_hi > SHORTLIST)
        return jnp.logical_and(p < max_passes, jnp.max(jnp.where(wide, 1, 0)) > 0)

    st = lax.while_loop(narrowing, functools.partial(search_pass, count_ge),
                        (jnp.int32(0), lo0, hi0, c_lo0, c_hi0, done0))

    p1, lo1, hi1, c_lo1, c_hi1, done1 = st
    hi_f = _key_to_float(hi1)
    short_ref[...] = jnp.full(short_ref.shape, -jnp.inf, jnp.float32)

    def shortlist_chunk(c, _):
        for r in range(0, ck, COUNT_ROWS):
            x = s_ref[pl.ds(pl.multiple_of(c * ck + r, COUNT_ROWS), COUNT_ROWS), :]
            y = jnp.where(x < hi_f, x, -jnp.inf)
            first = short_ref[r:r + COUNT_ROWS, :]
            second = short_ref[ck + r:ck + r + COUNT_ROWS, :]
            short_ref[r:r + COUNT_ROWS, :] = jnp.maximum(first, y)
            short_ref[ck + r:ck + r + COUNT_ROWS, :] = jnp.maximum(second, jnp.minimum(first, y))
        return 0

    lax.fori_loop(0, n_chunks, shortlist_chunk, 0)
    lo_f = _key_to_float(lo1)
    listed = count([lambda x: x >= lo_f], short_ref, short_ref.shape[0] // ck)[0]
    complete = listed == c_lo1 - c_hi1

    def count_short(cf):
        return c_hi1 + count([lambda x: x >= cf], short_ref, short_ref.shape[0] // ck)[0]

    st = lax.while_loop(searching, functools.partial(search_pass, count_short),
                        (p1, lo1, hi1, c_lo1, c_hi1, jnp.where(complete, done1, 1)))
    p2, lo2, hi2, c_lo2, c_hi2, done2 = st
    pick = lambda a, b: jnp.where(complete, a, b)

    st = lax.while_loop(searching, functools.partial(search_pass, count_ge),
                        (p2, pick(lo2, lo1), pick(hi2, hi1), pick(c_lo2, c_lo1), pick(c_hi2, c_hi1),
                         pick(done2, done1)))
    _, lo, _, c_lo, c_hi, _ = st
    thr = _key_to_float(lo)

    need = jnp.where(c_lo > n_sel, n_sel - c_hi, float(seq))
    rt = RANK_TILE
    earlier = (lax.broadcasted_iota(jnp.int32, (rt, rt), 1)
               < lax.broadcasted_iota(jnp.int32, (rt, rt), 0)).astype(jnp.bfloat16)

    def mask_chunk(c, ties_before):
        for t0 in range(0, ck, rt):
            c0 = pl.multiple_of(c * ck + t0, rt)
            tie = jnp.where(s_ref[pl.ds(c0, rt), :] == thr, 1.0, 0.0)
            rank = ties_before + jnp.dot(earlier, tie.astype(jnp.bfloat16),
                                         preferred_element_type=jnp.float32)
            for r in range(0, rt, SEL_ROWS):
                r0 = pl.multiple_of(c0 + r, SEL_ROWS)
                x = s_ref[pl.ds(r0, SEL_ROWS), :]
                kept_tie = jnp.logical_and(x == thr, rank[r:r + SEL_ROWS] < need)
                keep = jnp.logical_and(jnp.logical_or(x > thr, kept_tie), r0 + key_iota <= q_pos)
                mask_ref[:, pl.ds(r0, SEL_ROWS)] = jnp.where(keep, 0.0, NEG_MASK).T.astype(jnp.bfloat16)
            ties_before = ties_before + jnp.sum(_fold_rows(tie, jnp.add), axis=0, keepdims=True)
        return ties_before

    lax.fori_loop(0, n_chunks, mask_chunk, jnp.zeros((1, tq), jnp.float32))

    def fill_chunk(c, _):
        r0 = pl.multiple_of(c * ck, ck)
        mask_ref[:, pl.ds(r0, ck)] = jnp.full((tq, ck), NEG_MASK, jnp.bfloat16)
        return 0

    lax.fori_loop(n_chunks, seq // ck, fill_chunk, 0)


def _select(k_idx, qi_t, w_t, n_sel):
    s = k_idx.shape[0]
    tq = SEL_TILE
    assert n_sel <= RANK_TILE and s % SEL_CHUNK == 0 and SEL_CHUNK % RANK_TILE == 0
    return pl.pallas_call(
        functools.partial(_select_kernel, seq=s, n_sel=n_sel),
        grid=(s // tq,),
        in_specs=[pl.BlockSpec(k_idx.shape, lambda i: (0, 0)),
                  pl.BlockSpec((qi_t.shape[0], tq), lambda i: (0, i)),
                  pl.BlockSpec((IDX_HEADS, tq), lambda i: (0, i))],
        out_specs=pl.BlockSpec((tq, s), lambda i: (i, 0)),
        out_shape=jax.ShapeDtypeStruct((s, s), jnp.bfloat16),
        scratch_shapes=[pltpu.VMEM((s, tq), jnp.float32), pltpu.VMEM((RANK_TILE, tq), jnp.float32),
                        pltpu.VMEM((2 * SEL_CHUNK, tq), jnp.float32)],
        compiler_params=_cparams(("parallel",)),
        name="dsa_select",
    )(k_idx, qi_t, w_t)


def _qk(a, b):
    return lax.dot_general(a, b, (((1,), (1,)), ((), ())), preferred_element_type=jnp.float32)


def _pair_tables(nq):
    qt = np.concatenate([np.full(q + 1, q, np.int32) for q in range(nq)])
    st = np.concatenate([np.arange(q + 1, dtype=np.int32) for q in range(nq)])
    return jnp.asarray(qt), jnp.asarray(st)


def _softmax_step(h, s, v, m_ref, acc_ref):
    ts = s.shape[1]
    m_prev = m_ref[h]
    m_new = jnp.maximum(m_prev, jnp.max(s, axis=-1, keepdims=True))
    alpha = jnp.exp2(m_prev - m_new)
    pm = jnp.exp2(s - jnp.tile(m_new, (1, ts // LANES))).astype(jnp.bfloat16)
    v1 = jnp.concatenate([v, jnp.ones((ts, HEAD_DIM), jnp.bfloat16)], axis=1)
    acc_ref[h] = (jnp.tile(alpha, (1, 2 * HEAD_DIM // LANES)) * acc_ref[h]
                  + jnp.dot(pm, v1, preferred_element_type=jnp.float32))
    m_ref[h] = m_new


def _init_softmax(m_ref, acc_ref):
    m_ref[...] = jnp.full(m_ref.shape, NEG_MASK, jnp.float32)
    acc_ref[...] = jnp.zeros(acc_ref.shape, jnp.float32)


def _softmax_result(h, acc_ref):
    return acc_ref[h, :, :HEAD_DIM] / acc_ref[h, :, HEAD_DIM:]


def _att_scratch(tq):
    return [pltpu.VMEM((N_HEADS, tq, LANES), jnp.float32),
            pltpu.VMEM((N_HEADS, tq, 2 * HEAD_DIM), jnp.float32)]


def _dsa_kernel(qt_ref, st_ref, q_ref, k_ref, v_ref, mask_ref, bias_ref, o_ref, m_ref, acc_ref):
    p = pl.program_id(0)
    qi, si = qt_ref[p], st_ref[p]
    tq, ts = mask_ref.shape
    bt = bias_ref.shape[-1]
    lag00 = (qi * tq - si * ts) // bt
    lags = [[lag00 + a - b for b in range(ts // bt)] for a in range(tq // bt)]
    quad = [[jnp.where(lag == 0, 0, jnp.where(lag == 1, 1, 2)) for lag in row] for row in lags]
    near = lags[0][-1] <= 1
    last = (si + 1) * ts >= (qi + 1) * tq

    @pl.when(si == 0)
    def _():
        _init_softmax(m_ref, acc_ref)

    def sweep(with_bias):
        maskf = mask_ref[...].astype(jnp.float32)
        for h in range(N_HEADS):
            hs = slice(h * HEAD_DIM, (h + 1) * HEAD_DIM)
            s = _qk(q_ref[:, hs], k_ref[:, hs]) + maskf
            if with_bias:
                s = s + jnp.concatenate(
                    [jnp.concatenate([bias_ref[h, idx] for idx in row], axis=1) for row in quad], axis=0)
            _softmax_step(h, s, v_ref[:, hs], m_ref, acc_ref)

    pl.when(near)(functools.partial(sweep, True))
    pl.when(jnp.logical_not(near))(functools.partial(sweep, False))

    @pl.when(last)
    def _():
        for h in range(N_HEADS):
            hs = slice(h * HEAD_DIM, (h + 1) * HEAD_DIM)
            o_ref[:, hs] = _softmax_result(h, acc_ref).astype(o_ref.dtype)


def _dsa(proj, mask, bias_tiles):
    s = proj.shape[0]
    tq, ts = DSA_Q_TILE, ATT_TILE
    assert tq % ts == 0 and ts % BIAS_TILE == 0 and bias_tiles.shape[1:] == (3, BIAS_TILE, BIAS_TILE)
    r = tq // ts
    nq = s // tq
    qt = jnp.asarray(np.concatenate([np.full(r * (q + 1), q, np.int32) for q in range(nq)]))
    st = jnp.asarray(np.concatenate([np.arange(r * (q + 1), dtype=np.int32) for q in range(nq)]))
    w = GROUP_WIDTH
    grid_spec = pltpu.PrefetchScalarGridSpec(
        num_scalar_prefetch=2,
        grid=(qt.shape[0],),
        in_specs=[pl.BlockSpec((tq, w), lambda p, qt, st: (qt[p], 0)),
                  pl.BlockSpec((ts, w), lambda p, qt, st: (st[p], 1)),
                  pl.BlockSpec((ts, w), lambda p, qt, st: (st[p], 2)),
                  pl.BlockSpec((tq, ts), lambda p, qt, st: (qt[p], st[p])),
                  pl.BlockSpec(bias_tiles.shape, lambda p, qt, st: (0, 0, 0, 0))],
        out_specs=pl.BlockSpec((tq, w), lambda p, qt, st: (qt[p], 0)),
        scratch_shapes=_att_scratch(tq))
    return pl.pallas_call(
        _dsa_kernel, grid_spec=grid_spec,
        out_shape=jax.ShapeDtypeStruct((s, w), jnp.bfloat16),
        compiler_params=_cparams(("arbitrary",)),
        name="dsa_attention",
    )(qt, st, proj, proj, proj, mask, bias_tiles)


def _fox_kernel(qt_ref, st_ref, se_ref, live_ref, q_ref, k_ref, v_ref, g_ref, fq_ref, fs_ref, o_ref,
                m_ref, acc_ref):
    p = pl.program_id(0)
    qi, si = qt_ref[p], st_ref[p]
    tq, ts = q_ref.shape[0], k_ref.shape[0]

    @pl.when(si == 0)
    def _():
        _init_softmax(m_ref, acc_ref)

    def sweep(diagonal):
        if diagonal:
            causal = (lax.broadcasted_iota(jnp.int32, (tq, ts), 1)
                      <= lax.broadcasted_iota(jnp.int32, (tq, ts), 0))
        for h in range(N_HEADS):
            hs = slice(h * HEAD_DIM, (h + 1) * HEAD_DIM)
            decay = (fq_ref[h:h + 1, 0:1] - fs_ref[h:h + 1, :]) * LOG2E
            s = _qk(q_ref[:, hs], k_ref[:, hs]) + decay
            if diagonal:
                s = jnp.where(causal, s, NEG_MASK)
            _softmax_step(h, s, v_ref[:, hs], m_ref, acc_ref)

    pl.when(si == qi)(functools.partial(sweep, True))
    pl.when(jnp.logical_and(si != qi, live_ref[p] != 0))(functools.partial(sweep, False))

    @pl.when(si == qi)
    def _():
        for h in range(N_HEADS):
            hs = slice(h * HEAD_DIM, (h + 1) * HEAD_DIM)
            g = g_ref[:, hs].astype(jnp.float32)
            o_ref[:, hs] = (_softmax_result(h, acc_ref) / (1.0 + jnp.exp(-g))).astype(o_ref.dtype)


def _fox_schedule(sq_norms, f_rows, qt, st, t):
    n = sq_norms.shape[0] // t
    tile_norm = jnp.sqrt(jnp.max(sq_norms[:, :2 * N_HEADS].reshape(n, t, 2 * N_HEADS), axis=1)).T
    qn, kn = tile_norm[:N_HEADS], tile_norm[N_HEADS:]
    f = f_rows.reshape(N_HEADS, n, t)
    f_min, f_max = jnp.min(f, axis=-1), jnp.max(f, axis=-1)
    bound = (NORM_SLACK * qn[:, :, None] * (kn[:, None, :] + kn[:, :, None])
             - (f_min[:, None, :] - f_max[:, :, None]) * LOG2E)
    live_tiles = jnp.any(bound > -SKIP_LOG2, axis=0)
    live = jnp.logical_or(live_tiles[qt, st], qt == st)
    steps = jnp.arange(qt.shape[0], dtype=jnp.int32)
    next_live = lax.cummin(jnp.where(live, steps, qt.shape[0]), axis=0, reverse=True)
    return st[next_live], live.astype(jnp.int32)


def _fox(proj, sq_norms, f_rows):
    s = proj.shape[0]
    t = ATT_TILE
    qt, st = _pair_tables(s // t)
    se, live = _fox_schedule(sq_norms, f_rows, qt, st, t)
    w = GROUP_WIDTH
    grid_spec = pltpu.PrefetchScalarGridSpec(
        num_scalar_prefetch=4,
        grid=(qt.shape[0],),
        in_specs=[pl.BlockSpec((t, w), lambda p, qt, st, se, lv: (qt[p], 3)),
                  pl.BlockSpec((t, w), lambda p, qt, st, se, lv: (se[p], 4)),
                  pl.BlockSpec((t, w), lambda p, qt, st, se, lv: (se[p], 5)),
                  pl.BlockSpec((t, w), lambda p, qt, st, se, lv: (qt[p], 6)),
                  pl.BlockSpec((N_HEADS, t), lambda p, qt, st, se, lv: (0, qt[p])),
                  pl.BlockSpec((N_HEADS, t), lambda p, qt, st, se, lv: (0, se[p]))],
        out_specs=pl.BlockSpec((t, w), lambda p, qt, st, se, lv: (qt[p], 0)),
        scratch_shapes=_att_scratch(t))
    return pl.pallas_call(
        _fox_kernel, grid_spec=grid_spec,
        out_shape=jax.ShapeDtypeStruct((s, w), jnp.bfloat16),
        compiler_params=_cparams(("arbitrary",)),
        name="fox_attention",
    )(qt, st, se, live, proj, proj, proj, proj, f_rows, f_rows)


def _outproj_kernel(x_ref, oa_ref, ob_ref, wa_ref, wb_ref, gate_ref, o_ref):
    y = jnp.dot(oa_ref[...], wa_ref[...], preferred_element_type=jnp.float32)
    y = y + jnp.dot(ob_ref[...], wb_ref[...], preferred_element_type=jnp.float32)
    o_ref[...] = x_ref[...] + gate_ref[...] * y


def _outproj(x, o_a, o_b, w_a, w_b, gate):
    s, d = x.shape
    tm, tn = ROW_TILE, 1024
    w = GROUP_WIDTH
    return pl.pallas_call(
        _outproj_kernel,
        grid=(s // tm, d // tn),
        in_specs=[pl.BlockSpec((tm, tn), lambda i, j: (i, j)),
                  pl.BlockSpec((tm, w), lambda i, j: (i, 0)),
                  pl.BlockSpec((tm, w), lambda i, j: (i, 0)),
                  pl.BlockSpec((w, tn), lambda i, j: (0, j)),
                  pl.BlockSpec((w, tn), lambda i, j: (0, j)),
                  pl.BlockSpec((1, tn), lambda i, j: (0, j))],
        out_specs=pl.BlockSpec((tm, tn), lambda i, j: (i, j)),
        out_shape=jax.ShapeDtypeStruct((s, d), jnp.float32),
        compiler_params=_cparams(("parallel", "arbitrary")),
        name="outproj",
    )(x, o_a, o_b, w_a, w_b, gate)


def _conv(y_ref, cw_ref, cb_ref, tm):
    h = CONV_HALO
    return (cw_ref[2:3, :] * y_ref[h:h + tm, :] + cw_ref[1:2, :] * y_ref[h - 1:h - 1 + tm, :]
            + cw_ref[0:1, :] * y_ref[h - 2:h - 2 + tm, :] + cb_ref[...])


def _ffn_kernel(x_ref, xh_ref, g_ref, sh_ref, sc_ref, wg_ref, wv_ref, cwg_ref, cwv_ref, cbg_ref, cbv_ref,
                wd_ref, gate_ref, gf_ref, o_ref, h_ref, yg_ref, yv_ref, acc_ref):
    i, f = pl.program_id(0), pl.program_id(1)
    tm = x_ref.shape[0]

    @pl.when(f == 0)
    def _():
        halo = _norm_mod(xh_ref[...], g_ref[...], sh_ref[...], sc_ref[...])
        h_ref[0:CONV_HALO, :] = jnp.where(i > 0, halo, 0.0).astype(jnp.bfloat16)
        h_ref[CONV_HALO:, :] = _norm_mod(x_ref[...], g_ref[...], sh_ref[...], sc_ref[...]).astype(jnp.bfloat16)
        acc_ref[...] = jnp.zeros(acc_ref.shape, jnp.float32)

    hb = h_ref[...]
    yg_ref[...] = jnp.dot(hb, wg_ref[...], preferred_element_type=jnp.float32)
    yv_ref[...] = jnp.dot(hb, wv_ref[...], preferred_element_type=jnp.float32)
    ug = _conv(yg_ref, cwg_ref, cbg_ref, tm)
    uv = _conv(yv_ref, cwv_ref, cbv_ref, tm)
    a = (ug / (1.0 + jnp.exp(-ug))) * uv
    acc_ref[...] += jnp.dot(a.astype(jnp.bfloat16), wd_ref[...], preferred_element_type=jnp.float32)

    @pl.when(f == pl.num_programs(1) - 1)
    def _():
        x2 = x_ref[...] + gate_ref[...] * acc_ref[...]
        y = x2 * lax.rsqrt(jnp.mean(x2 * x2, axis=-1, keepdims=True) + EPS)
        o_ref[...] = y * gf_ref[...]


def _ffn(x, g, shift, scale, w_gate, w_val, cw_gate, cw_val, cb_gate, cb_val, w_down, gate, g_final):
    s, d = x.shape
    fp = w_down.shape[0]
    tm, tf = ROW_TILE, FF_TILE
    nf = fp // tf
    hb = tm // CONV_HALO
    row = lambda i, f: (0, 0)
    return pl.pallas_call(
        _ffn_kernel,
        grid=(s // tm, nf),
        in_specs=[pl.BlockSpec((tm, d), lambda i, f: (i, 0)),
                  pl.BlockSpec((CONV_HALO, d), lambda i, f: (jnp.maximum(i * hb - 1, 0), 0)),
                  pl.BlockSpec((1, d), row), pl.BlockSpec((1, d), row), pl.BlockSpec((1, d), row),
                  pl.BlockSpec((d, tf), lambda i, f: (0, f)),
                  pl.BlockSpec((d, tf), lambda i, f: (0, f)),
                  pl.BlockSpec((CONV_WIDTH, tf), lambda i, f: (0, f)),
                  pl.BlockSpec((CONV_WIDTH, tf), lambda i, f: (0, f)),
                  pl.BlockSpec((1, tf), lambda i, f: (0, f)),
                  pl.BlockSpec((1, tf), lambda i, f: (0, f)),
                  pl.BlockSpec((tf, d), lambda i, f: (f, 0)),
                  pl.BlockSpec((1, d), row), pl.BlockSpec((1, d), row)],
        out_specs=pl.BlockSpec((tm, d), lambda i, f: (i, 0)),
        out_shape=jax.ShapeDtypeStruct((s, d), jnp.float32),
        scratch_shapes=[pltpu.VMEM((tm + CONV_HALO, d), jnp.bfloat16),
                        pltpu.VMEM((tm + CONV_HALO, tf), jnp.float32),
                        pltpu.VMEM((tm + CONV_HALO, tf), jnp.float32),
                        pltpu.VMEM((tm, d), jnp.float32)],
        compiler_params=_cparams(("parallel", "arbitrary")),
        name="conv_ffn",
    )(x, x, g, shift, scale, w_gate, w_val, cw_gate, cw_val, cb_gate, cb_val, w_down, gate, g_final)


def _t5_bias_tiles(rel_bias, t):
    max_exact = N_BUCKETS // 2
    d = jnp.arange(2 * t, dtype=jnp.int32)
    df = jnp.maximum(d, 1).astype(jnp.float32)
    large = max_exact + (jnp.log(df / max_exact) / math.log(MAX_DISTANCE / max_exact)
                         * (N_BUCKETS - max_exact)).astype(jnp.int32)
    bucket = jnp.where(d < max_exact, d, jnp.minimum(large, N_BUCKETS - 1))
    rb = rel_bias.astype(jnp.float32)
    table = (rb[bucket] - rb[N_BUCKETS - 1]) * LOG2E
    period = 2 * t
    k = np.arange(period)
    tiles = []
    for off in (0, t):
        idx = np.clip(np.where(k < t, off - k, off + period - k), 0, 2 * t - 1)
        seq = jnp.tile(table[idx].T, (1, t))
        tiles.append(seq[:, :t * (period - 1)].reshape(N_HEADS, t, period - 1)[:, :, :t])
    return jnp.stack(tiles + [jnp.zeros_like(tiles[0])], axis=1)


def _pad_cols(a, n):
    return jnp.pad(a, ((0, 0), (0, n - a.shape[1])))


def kernel(x, c, rel_bias, w_ada, b_ada, g_attn, w_in, b_forget, w_out, g_mlp, w_up, conv_w, conv_b,
           w_down, g_final):
    b, s, d = x.shape
    assert b == 1 and s % ROW_TILE == 0 and s % DSA_Q_TILE == 0 and s % SEL_TILE == 0
    assert BIAS_TILE >= MAX_DISTANCE
    depth = w_ada.shape[0]
    assert depth == 1, "the fused MLP kernel applies the final norm, so it must be the last layer"
    d_ff = w_down.shape[1]
    ff_pad = -(-d_ff // FF_TILE) * FF_TILE
    n_sel = min(TOPK_MAX, s // 4)
    gw, nq = GROUP_WIDTH, IDX_HEADS * IDX_DIM
    bf = jnp.bfloat16
    xs = x[0]
    c_col = c.reshape(d, 1)
    bias_tiles = _t5_bias_tiles(rel_bias, BIAS_TILE)

    for l in range(depth):
        mod = _ada(c_col, w_ada[l], b_ada[l][None, :])
        shift_a, scale_a, gate_a, shift_m, scale_m, gate_m = [mod[:, k * d:(k + 1) * d] for k in range(6)]

        wi = w_in[l]
        o = np.cumsum([0, gw, gw, gw, nq, IDX_DIM, IDX_HEADS, gw, gw, gw, gw, N_HEADS])
        seg = lambda k: wi[:, o[k]:o[k + 1]]
        q_scale = HEAD_DIM ** -0.5 * LOG2E
        w_main = jnp.concatenate([seg(0) * q_scale, seg(1), seg(2),
                                  seg(6) * q_scale, seg(7), seg(8), seg(9)], axis=1).astype(bf)
        w_small = _pad_cols(jnp.concatenate([seg(3) * IDX_DIM ** -0.5, seg(4), seg(5), seg(10)], axis=1),
                            SMALL_WIDTH).astype(bf)

        proj, qi, small, sq_norms = _inproj(xs, g_attn[l][None, :], shift_a, scale_a, w_main, w_small)

        k_idx = small[:, :IDX_DIM].astype(bf)
        w_t = small[:, IDX_DIM:IDX_DIM + IDX_HEADS].T
        fb_t = small[:, IDX_DIM + IDX_HEADS:IDX_DIM + IDX_HEADS + N_HEADS].T
        f_rows = _fcum(fb_t.reshape(N_HEADS, s // LANES, LANES), b_forget[l]).reshape(N_HEADS, s)

        mask = _select(k_idx, qi.T, w_t, n_sel)
        o_a = _dsa(proj, mask, bias_tiles)
        o_b = _fox(proj, sq_norms, f_rows)

        wo = w_out[l].astype(bf)
        x1 = _outproj(xs, o_a, o_b, wo[:gw], wo[gw:], gate_a)

        halves = lambda a: (_pad_cols(a[:, :d_ff], ff_pad), _pad_cols(a[:, d_ff:], ff_pad))
        w_gate, w_val = halves(w_up[l].astype(bf))
        cw_gate, cw_val = halves(conv_w[l])
        cb_gate, cb_val = halves(conv_b[l][None, :])
        w_down_p = jnp.pad(w_down[l].astype(bf), ((0, ff_pad - d_ff), (0, 0)))

        xs = _ffn(x1, g_mlp[l][None, :], shift_m, scale_m, w_gate, w_val, cw_gate, cw_val, cb_gate, cb_val,
                  w_down_p, gate_m, g_final[None, :])

    return xs[None]
```
